```python
import math
import jax, jax.numpy as jnp
from jax import lax
import numpy as np

D_MODEL = 1024
BATCH = 8
SEQ = 8192
DEPTH = 4

N_MIXERS = 2
DA_HEADS = 8
DA_HEAD_DIM = 64
DA_QK_WIDTH = 2 * DA_HEADS * DA_HEAD_DIM
DA_V_WIDTH = DA_HEADS * 2 * DA_HEAD_DIM
SB_HEADS = 16
SB_HEAD_DIM = 64
SB_WIDTH = SB_HEADS * SB_HEAD_DIM
ROPE_DIM = DA_HEAD_DIM // 4
ROPE_THETA = 500000.0
D_FF = -(-8 * D_MODEL // (3 * 256)) * 256
QBLOCK = 128
DEEPNORM_ALPHA = (2 * DEPTH) ** 0.25
DEEPNORM_BETA = (8 * DEPTH) ** -0.25
LN_EPS = 1e-5
SUBLN_EPS = 1e-5
NEG_INF = -1e30
N_DA_LAYERS = (DEPTH + 1) // 2
N_SB_LAYERS = DEPTH // 2

kernel_name = "hybrid_diffattn_stickbreaking_deepnorm"


def lambda_init(layer_idx):
    return 0.8 - 0.6 * math.exp(-0.3 * layer_idx)


def layer_norm(x, g, b):
    xf = x.astype(jnp.float32)
    mu = jnp.mean(xf, axis=-1, keepdims=True)
    xc = xf - mu
    var = jnp.mean(xc * xc, axis=-1, keepdims=True)
    y = xc * lax.rsqrt(var + LN_EPS) * g.astype(jnp.float32) + b.astype(jnp.float32)
    return y.astype(x.dtype)


def rotary_tables(positions, dtype):
    inv_freq = ROPE_THETA ** (-jnp.arange(0, ROPE_DIM, 2, dtype=jnp.float32) / ROPE_DIM)
    ang = positions.astype(jnp.float32)[..., None] * inv_freq
    return jnp.cos(ang)[:, :, None, :].astype(dtype), jnp.sin(ang)[:, :, None, :].astype(dtype)


def apply_partial_rope(t, cos, sin):
    half = ROPE_DIM // 2
    t1 = t[..., :half]
    t2 = t[..., half:ROPE_DIM]
    return jnp.concatenate([t1 * cos - t2 * sin, t2 * cos + t1 * sin, t[..., ROPE_DIM:]], axis=-1)


def causal_mask(blk, n_keys, strict):
    q_pos = blk * QBLOCK + jnp.arange(QBLOCK)[:, None]
    k_pos = jnp.arange(n_keys)[None, :]
    return (k_pos < q_pos) if strict else (k_pos <= q_pos)


def differential_attention(x, cos, sin, w_qkv, w_o, lq1, lk1, lq2, lk2, subln_g, lam_init):
    b, s, _ = x.shape
    qkv = x @ w_qkv
    q, k, v = jnp.split(qkv, [DA_QK_WIDTH, 2 * DA_QK_WIDTH], axis=-1)
    q = apply_partial_rope(q.reshape(b, s, 2 * DA_HEADS, DA_HEAD_DIM), cos, sin)
    k = apply_partial_rope(k.reshape(b, s, 2 * DA_HEADS, DA_HEAD_DIM), cos, sin)
    v = v.reshape(b, s, DA_HEADS, 2 * DA_HEAD_DIM)
    f32 = jnp.float32
    lam = (jnp.exp(jnp.sum(lq1.astype(f32) * lk1.astype(f32)))
           - jnp.exp(jnp.sum(lq2.astype(f32) * lk2.astype(f32))) + lam_init)
    scale = DA_HEAD_DIM ** -0.5
    outs = []
    for blk in range(s // QBLOCK):
        lo, hi = blk * QBLOCK, (blk + 1) * QBLOCK
        qb, kb, vb = q[:, lo:hi], k[:, :hi], v[:, :hi]
        sc = jnp.einsum('bqhd,bkhd->bhqk', qb, kb).astype(f32) * scale
        sc = jnp.where(causal_mask(blk, hi, strict=False), sc, NEG_INF)
        p = jax.nn.softmax(sc, axis=-1).reshape(b, DA_HEADS, 2, QBLOCK, hi)
        a = p[:, :, 0] - lam * p[:, :, 1]
        outs.append(jnp.einsum('bhqk,bkhd->bqhd', a.astype(vb.dtype), vb))
    o = jnp.concatenate(outs, axis=1).astype(f32)
    o = o * lax.rsqrt(jnp.mean(o * o, axis=-1, keepdims=True) + SUBLN_EPS)
    o = o * subln_g.astype(f32) * (1.0 - lam_init)
    return o.reshape(b, s, DA_V_WIDTH).astype(x.dtype) @ w_o


def stick_breaking_attention(x, w_qkv, w_o):
    b, s, _ = x.shape
    qkv = x @ w_qkv
    q, k, v = jnp.split(qkv, [SB_WIDTH, 2 * SB_WIDTH], axis=-1)
    q = q.reshape(b, s, SB_HEADS, SB_HEAD_DIM)
    k = k.reshape(b, s, SB_HEADS, SB_HEAD_DIM)
    v = v.reshape(b, s, SB_HEADS, SB_HEAD_DIM)
    scale = SB_HEAD_DIM ** -0.5
    outs = []
    for blk in range(s // QBLOCK):
        lo, hi = blk * QBLOCK, (blk + 1) * QBLOCK
        qb, kb, vb = q[:, lo:hi], k[:, :hi], v[:, :hi]
        z = jnp.einsum('bqhd,bkhd->bhqk', qb, kb).astype(jnp.float32) * scale
        mask = causal_mask(blk, hi, strict=True)
        log_beta = jax.nn.log_sigmoid(z)
        log_1m = jnp.where(mask, jax.nn.log_sigmoid(-z), 0.0)
        later = lax.cumsum(log_1m, axis=3, reverse=True) - log_1m
        a = jnp.where(mask, jnp.exp(log_beta + later), 0.0)
        outs.append(jnp.einsum('bhqk,bkhd->bqhd', a.astype(vb.dtype), vb))
    o = jnp.concatenate(outs, axis=1)
    return o.reshape(b, s, SB_WIDTH) @ w_o


def swiglu(x, w_gate_up, w_down):
    g, u = jnp.split(x @ w_gate_up, [D_FF], axis=-1)
    return (jax.nn.silu(g) * u) @ w_down


def setup_inputs(seed: int = 0) -> dict:
    key = jax.random.key(seed)
    ks = jax.random.split(key, 20)
    nrm = jax.random.normal
    f32 = jnp.float32
    x = nrm(ks[0], (BATCH, SEQ, D_MODEL), f32)
    offset = jax.random.randint(ks[1], (BATCH, 1), 0, SEQ, dtype=jnp.int32)
    positions = offset + jnp.arange(SEQ, dtype=jnp.int32)[None, :]
    ln_attn_g = 1.0 + 0.02 * nrm(ks[2], (DEPTH, D_MODEL), f32)
    ln_attn_b = 0.02 * nrm(ks[3], (DEPTH, D_MODEL), f32)
    ln_ffn_g = 1.0 + 0.02 * nrm(ks[4], (DEPTH, D_MODEL), f32)
    ln_ffn_b = 0.02 * nrm(ks[5], (DEPTH, D_MODEL), f32)
    w_qkv_diff = nrm(ks[6], (N_DA_LAYERS, D_MODEL, 2 * DA_QK_WIDTH + DA_V_WIDTH), f32) * D_MODEL ** -0.5
    w_o_diff = nrm(ks[7], (N_DA_LAYERS, DA_V_WIDTH, D_MODEL), f32) * (DA_V_WIDTH ** -0.5 * DEEPNORM_BETA)
    lambda_q1 = 0.1 * nrm(ks[8], (N_DA_LAYERS, DA_HEAD_DIM), f32)
    lambda_k1 = 0.1 * nrm(ks[9], (N_DA_LAYERS, DA_HEAD_DIM), f32)
    lambda_q2 = 0.1 * nrm(ks[10], (N_DA_LAYERS, DA_HEAD_DIM), f32)
    lambda_k2 = 0.1 * nrm(ks[11], (N_DA_LAYERS, DA_HEAD_DIM), f32)
    subln_g = 1.0 + 0.02 * nrm(ks[12], (N_DA_LAYERS, 2 * DA_HEAD_DIM), f32)
    w_qkv_sb = nrm(ks[13], (N_SB_LAYERS, D_MODEL, 3 * SB_WIDTH), f32) * D_MODEL ** -0.5
    w_o_sb = nrm(ks[14], (N_SB_LAYERS, SB_WIDTH, D_MODEL), f32) * (SB_WIDTH ** -0.5 * DEEPNORM_BETA)
    w_gate_up = nrm(ks[15], (DEPTH, D_MODEL, 2 * D_FF), f32) * D_MODEL ** -0.5
    w_down = nrm(ks[16], (DEPTH, D_FF, D_MODEL), f32) * (D_FF ** -0.5 * DEEPNORM_BETA)
    return {"x": x, "positions": positions,
            "ln_attn_g": ln_attn_g, "ln_attn_b": ln_attn_b,
            "ln_ffn_g": ln_ffn_g, "ln_ffn_b": ln_ffn_b,
            "w_qkv_diff": w_qkv_diff, "w_o_diff": w_o_diff,
            "lambda_q1": lambda_q1, "lambda_k1": lambda_k1,
            "lambda_q2": lambda_q2, "lambda_k2": lambda_k2,
            "subln_g": subln_g,
            "w_qkv_sb": w_qkv_sb, "w_o_sb": w_o_sb,
            "w_gate_up": w_gate_up, "w_down": w_down}


def reference(x, positions, ln_attn_g, ln_attn_b, ln_ffn_g, ln_ffn_b,
              w_qkv_diff, w_o_diff, lambda_q1, lambda_k1, lambda_q2, lambda_k2,
              subln_g, w_qkv_sb, w_o_sb, w_gate_up, w_down):
    cos, sin = rotary_tables(positions, x.dtype)
    for i in range(DEPTH):
        j = i // N_MIXERS
        if i % N_MIXERS == 0:
            h = differential_attention(x, cos, sin, w_qkv_diff[j], w_o_diff[j],
                                       lambda_q1[j], lambda_k1[j], lambda_q2[j], lambda_k2[j],
                                       subln_g[j], lambda_init(i))
        else:
            h = stick_breaking_attention(x, w_qkv_sb[j], w_o_sb[j])
        x = layer_norm(DEEPNORM_ALPHA * x + h, ln_attn_g[i], ln_attn_b[i])
        x = layer_norm(DEEPNORM_ALPHA * x + swiglu(x, w_gate_up[i], w_down[i]),
                       ln_ffn_g[i], ln_ffn_b[i])
    return x
```

```python
import functools
import math

import jax
import jax.numpy as jnp
from jax import lax
from jax.experimental import pallas as pl
from jax.experimental.pallas import tpu as pltpu

F32 = jnp.float32
BF16 = jnp.bfloat16

HEAD_DIM = 64
PAIR = 2 * HEAD_DIM
ROPE_DIM = HEAD_DIM // 4
ROPE_HALF = ROPE_DIM // 2
ROPE_THETA = 500000.0
LN_EPS = 1e-5
SUBLN_EPS = 1e-5
NEG_INF = -1e30
SB_SKIP_LOG = -105.0

TM_DENSE = 512
TQ_DA, TK_DA = 512, 512
TQ_SB, TK_SB = 256, 256
FF_CHUNKS = 2
VMEM_LIMIT = 56 * 1024 * 1024


def _lambda_init(layer_idx):
    return 0.8 - 0.6 * math.exp(-0.3 * layer_idx)


def _nt(a, b):
    return lax.dot_general(a, b, (((1,), (1,)), ((), ())), preferred_element_type=F32)


def _params(*sem):
    return pltpu.CompilerParams(dimension_semantics=sem, vmem_limit_bytes=VMEM_LIMIT)


def _const_spec(shape):
    nd = len(shape)
    return pl.BlockSpec(shape, lambda *_: (0,) * nd, pipeline_mode=pl.Buffered(1))


def _rope_table_kernel(pos_ref, invf_ref, cos_ref, sin_ref):
    ang = pos_ref[0].astype(F32) * invf_ref[...]
    cos_ref[0] = jnp.cos(ang)
    sin_ref[0] = jnp.sin(ang)


def _rope_tables(positions):
    b, s = positions.shape
    inv_freq = ROPE_THETA ** (-jnp.arange(0, ROPE_DIM, 2, dtype=F32) / ROPE_DIM)
    out = jax.ShapeDtypeStruct((b, ROPE_HALF, s), F32)
    return pl.pallas_call(
        _rope_table_kernel,
        grid=(b,),
        in_specs=[pl.BlockSpec((1, 1, s), lambda i: (i, 0, 0)),
                  _const_spec((ROPE_HALF, 1))],
        out_specs=[pl.BlockSpec((1, ROPE_HALF, s), lambda i: (i, 0, 0))] * 2,
        out_shape=[out, out],
        compiler_params=_params("arbitrary"),
        name="rope_tables",
    )(positions.reshape(b, 1, s), inv_freq.reshape(ROPE_HALF, 1))


def _rope_t(t, cos, sin):
    d, n = t.shape
    t4 = t.reshape(d // HEAD_DIM, HEAD_DIM // ROPE_HALF, ROPE_HALF, n)
    t1, t2 = t4[:, 0], t4[:, 1]
    r1 = t1 * cos - t2 * sin
    r2 = t2 * cos + t1 * sin
    out = jnp.concatenate([r1[:, None], r2[:, None], t4[:, 2:]], axis=1)
    return out.reshape(d, n)


def _qkv_kernel(*refs, rope):
    if rope:
        x_ref, wq_ref, wk_ref, wv_ref, cos_ref, sin_ref, q_ref, k_ref, v_ref = refs
    else:
        x_ref, wq_ref, wk_ref, wv_ref, q_ref, k_ref, v_ref = refs
    xb = x_ref[0].astype(BF16)
    qt = _nt(wq_ref[...], xb)
    v_ref[0] = _nt(wv_ref[...], xb).astype(BF16)
    if rope:
        cos, sin = cos_ref[0], sin_ref[0]
        q_ref[0] = _rope_t(qt, cos, sin).astype(BF16)
        kt = _rope_t(_nt(wk_ref[...], xb), cos, sin)
        k_ref[0] = kt.T.astype(BF16)
    else:
        q_ref[0] = qt.astype(BF16)
        k_ref[0] = jnp.dot(xb, wk_ref[...], preferred_element_type=F32).astype(BF16)


def _qkv_proj(x, w_qkv, scale, tables):
    b, s, d = x.shape
    tm = min(TM_DENSE, s)
    rope = tables is not None
    wq_t = (w_qkv[:, :d] * scale).T.astype(BF16)
    wv_t = w_qkv[:, 2 * d:].T.astype(BF16)
    wk = w_qkv[:, d:2 * d]
    wk = (wk.T if rope else wk).astype(BF16)
    tok = pl.BlockSpec((1, tm, d), lambda i, j: (i, j, 0))
    tok_t = pl.BlockSpec((1, d, tm), lambda i, j: (i, 0, j))
    in_specs = [tok] + [_const_spec((d, d))] * 3
    args = [x, wq_t, wk, wv_t]
    if rope:
        in_specs += [pl.BlockSpec((1, ROPE_HALF, tm), lambda i, j: (i, 0, j))] * 2
        args += list(tables)
    return pl.pallas_call(
        functools.partial(_qkv_kernel, rope=rope),
        grid=(b, s // tm),
        in_specs=in_specs,
        out_specs=[tok_t, tok, tok_t],
        out_shape=[jax.ShapeDtypeStruct((b, d, s), BF16),
                   jax.ShapeDtypeStruct((b, s, d), BF16),
                   jax.ShapeDtypeStruct((b, d, s), BF16)],
        compiler_params=_params("arbitrary", "arbitrary"),
        name="qkv_rope" if rope else "qkv_plain",
    )(*args)


def _split_pair(qt):
    row = lax.broadcasted_iota(jnp.int32, qt.shape, 0)
    zero = jnp.zeros_like(qt)
    return jnp.where(row < HEAD_DIM, qt, zero), jnp.where(row >= HEAD_DIM, qt, zero)


def _da_attn_kernel(q_ref, k_ref, v_ref, lq1_ref, lk1_ref, lq2_ref, lk2_ref, g_ref, o_ref,
                    m_ref, l_ref, acc_ref, *, tq, tk, lam_init):
    qi = pl.program_id(2)
    q_sub = _split_pair(q_ref[0])

    m_ref[...] = jnp.full(m_ref.shape, NEG_INF, F32)
    l_ref[...] = jnp.zeros(l_ref.shape, F32)
    acc_ref[...] = jnp.zeros(acc_ref.shape, F32)

    def tile(j, masked):
        start = pl.multiple_of(j * tk, tk)
        kt = k_ref[0, pl.ds(start, tk), :]
        vt = v_ref[0, :, pl.ds(start, tk)]
        for c in range(2):
            s = jnp.dot(kt, q_sub[c], preferred_element_type=F32)
            if masked:
                key = lax.broadcasted_iota(jnp.int32, (tk, tq), 0)
                qry = lax.broadcasted_iota(jnp.int32, (tk, tq), 1)
                s = jnp.where(key <= qry, s, NEG_INF)
            m_old = m_ref[c]
            m_new = jnp.maximum(m_old, jnp.max(s, axis=0, keepdims=True))
            alpha = jnp.exp(m_old - m_new)
            p = jnp.exp(s - m_new)
            l_ref[c] = alpha * l_ref[c] + jnp.sum(p, axis=0, keepdims=True)
            acc_ref[c] = alpha * acc_ref[c] + jnp.dot(vt, p.astype(BF16),
                                                      preferred_element_type=F32)
            m_ref[c] = m_new

    def body(j, carry):
        tile(j, masked=False)
        return carry

    lax.fori_loop(0, qi, body, 0)
    tile(qi, masked=True)

    lam = (jnp.exp(jnp.sum(lq1_ref[...] * lk1_ref[...], axis=1, keepdims=True))
           - jnp.exp(jnp.sum(lq2_ref[...] * lk2_ref[...], axis=1, keepdims=True))
           + lam_init)
    o = acc_ref[0] / l_ref[0] - lam * (acc_ref[1] / l_ref[1])
    o = o * lax.rsqrt(jnp.mean(o * o, axis=0, keepdims=True) + SUBLN_EPS)
    o = o * g_ref[...] * (1.0 - lam_init)
    o_ref[0] = o.T.astype(o_ref.dtype)


def _da_attention(qt, k, vt, lq1, lk1, lq2, lk2, subln_g, lam_init):
    b, d, s = qt.shape
    tq = tk = min(TQ_DA, s)
    heads = d // PAIR
    vec = lambda a: a.reshape(1, HEAD_DIM).astype(F32)
    return pl.pallas_call(
        functools.partial(_da_attn_kernel, tq=tq, tk=tk, lam_init=lam_init),
        grid=(b, heads, s // tq),
        in_specs=[pl.BlockSpec((1, PAIR, tq), lambda i, h, j: (i, h, j)),
                  pl.BlockSpec((1, s, PAIR), lambda i, h, j: (i, 0, h)),
                  pl.BlockSpec((1, PAIR, s), lambda i, h, j: (i, h, 0)),
                  _const_spec((1, HEAD_DIM)), _const_spec((1, HEAD_DIM)),
                  _const_spec((1, HEAD_DIM)), _const_spec((1, HEAD_DIM)),
                  _const_spec((PAIR, 1))],
        out_specs=pl.BlockSpec((1, tq, PAIR), lambda i, h, j: (i, j, h)),
        out_shape=jax.ShapeDtypeStruct((b, s, d), BF16),
        scratch_shapes=[pltpu.VMEM((2, 1, tq), F32), pltpu.VMEM((2, 1, tq), F32),
                        pltpu.VMEM((2, PAIR, tq), F32)],
        compiler_params=_params("arbitrary", "arbitrary", "arbitrary"),
        name="da_attention",
    )(qt, k, vt, vec(lq1), vec(lk1), vec(lq2), vec(lk2),
      subln_g.reshape(PAIR, 1).astype(F32))


def _sb_attn_kernel(q_ref, k_ref, v_ref, tri_ref, o_ref, carry_ref, acc_ref, *, tq, tk):
    qi = pl.program_id(2)
    q_sub = _split_pair(q_ref[0])
    tri = tri_ref[...]
    n_diag = tq // tk

    carry_ref[...] = jnp.zeros(carry_ref.shape, F32)
    acc_ref[...] = jnp.zeros(acc_ref.shape, F32)

    def tile(j, masked):
        start = pl.multiple_of(j * tk, tk)
        kt = k_ref[0, pl.ds(start, tk), :]
        vt = v_ref[0, :, pl.ds(start, tk)]
        if masked:
            key = start + lax.broadcasted_iota(jnp.int32, (tk, tq), 0)
            qry = qi * tq + lax.broadcasted_iota(jnp.int32, (tk, tq), 1)
            mask = key < qry
        for c in range(2):
            z = jnp.dot(kt, q_sub[c], preferred_element_type=F32)
            sp = jnp.maximum(z, 0.0) + jnp.log(1.0 + jnp.exp(-jnp.abs(z)))
            log_1m = -sp
            if masked:
                log_1m = jnp.where(mask, log_1m, 0.0)
            hi = log_1m.astype(BF16)
            lo = (log_1m - hi.astype(F32)).astype(BF16)
            later = (jnp.dot(tri, hi, preferred_element_type=F32)
                     + jnp.dot(tri, lo, preferred_element_type=F32))
            carry = carry_ref[c]
            a = jnp.exp((z - sp) + (later + carry))
            if masked:
                a = jnp.where(mask, a, 0.0)
            vc = vt[c * HEAD_DIM:(c + 1) * HEAD_DIM]
            acc_ref[c] += jnp.dot(vc, a.astype(BF16), preferred_element_type=F32)
            carry_ref[c] = carry + later[0:1] + log_1m[0:1]

    for t in range(n_diag):
        tile(qi * n_diag + (n_diag - 1 - t), masked=True)

    def live():
        return jnp.max(jnp.maximum(carry_ref[0], carry_ref[1])) > SB_SKIP_LOG

    def cond(state):
        j, alive = state
        return jnp.logical_and(j >= 0, alive)

    def body(state):
        j, _ = state
        tile(j, masked=False)
        return j - 1, live()

    lax.while_loop(cond, body, (qi * n_diag - 1, live()))

    o = jnp.concatenate([acc_ref[0], acc_ref[1]], axis=0)
    o_ref[0] = o.T.astype(o_ref.dtype)


def _sb_attention(qt, k, vt):
    b, d, s = qt.shape
    tq, tk = min(TQ_SB, s), min(TK_SB, s)
    r = lax.broadcasted_iota(jnp.int32, (tk, tk), 0)
    c = lax.broadcasted_iota(jnp.int32, (tk, tk), 1)
    tri = (c > r).astype(BF16)
    return pl.pallas_call(
        functools.partial(_sb_attn_kernel, tq=tq, tk=tk),
        grid=(b, d // PAIR, s // tq),
        in_specs=[pl.BlockSpec((1, PAIR, tq), lambda i, h, j: (i, h, j)),
                  pl.BlockSpec((1, s, PAIR), lambda i, h, j: (i, 0, h)),
                  pl.BlockSpec((1, PAIR, s), lambda i, h, j: (i, h, 0)),
                  _const_spec((tk, tk))],
        out_specs=pl.BlockSpec((1, tq, PAIR), lambda i, h, j: (i, j, h)),
        out_shape=jax.ShapeDtypeStruct((b, s, d), BF16),
        scratch_shapes=[pltpu.VMEM((2, 1, tq), F32), pltpu.VMEM((2, HEAD_DIM, tq), F32)],
        compiler_params=_params("arbitrary", "arbitrary", "arbitrary"),
        name="sb_attention",
    )(qt, k, vt, tri)


def _layer_norm(y, g, b):
    mu = jnp.mean(y, axis=-1, keepdims=True)
    yc = y - mu
    var = jnp.mean(yc * yc, axis=-1, keepdims=True)
    return yc * lax.rsqrt(var + LN_EPS) * g + b


def _proj_ln_kernel(h_ref, x_ref, w_ref, g_ref, b_ref, o_ref, *, alpha):
    y = alpha * x_ref[...] + jnp.dot(h_ref[...], w_ref[...], preferred_element_type=F32)
    o_ref[...] = _layer_norm(y, g_ref[...], b_ref[...])


def _proj_ln(h, x, w_o, g, b, alpha):
    n, d = x.shape
    tm = min(TM_DENSE, n)
    tok = pl.BlockSpec((tm, d), lambda i: (i, 0))
    return pl.pallas_call(
        functools.partial(_proj_ln_kernel, alpha=alpha),
        grid=(n // tm,),
        in_specs=[tok, tok, _const_spec((d, d)), _const_spec((1, d)), _const_spec((1, d))],
        out_specs=tok,
        out_shape=jax.ShapeDtypeStruct((n, d), F32),
        compiler_params=_params("arbitrary"),
        name="proj_ln",
    )(h, x, w_o.astype(BF16), g.reshape(1, d), b.reshape(1, d))


def _ffn_ln_kernel(x_ref, wgu_ref, wd_ref, g_ref, b_ref, o_ref, act_ref, *, alpha, d_ff):
    x = x_ref[...]
    xb = x.astype(BF16)
    ch = d_ff // FF_CHUNKS
    for c in range(FF_CHUNKS):
        gate = jnp.dot(xb, wgu_ref[:, c * ch:(c + 1) * ch], preferred_element_type=F32)
        up = jnp.dot(xb, wgu_ref[:, d_ff + c * ch:d_ff + (c + 1) * ch],
                     preferred_element_type=F32)
        act_ref[:, c * ch:(c + 1) * ch] = (gate * jax.nn.sigmoid(gate) * up).astype(BF16)
    y = alpha * x + jnp.dot(act_ref[...], wd_ref[...], preferred_element_type=F32)
    o_ref[...] = _layer_norm(y, g_ref[...], b_ref[...])


def _ffn_ln(x, w_gate_up, w_down, g, b, alpha):
    n, d = x.shape
    d_ff = w_down.shape[0]
    tm = min(TM_DENSE, n)
    tok = pl.BlockSpec((tm, d), lambda i: (i, 0))
    return pl.pallas_call(
        functools.partial(_ffn_ln_kernel, alpha=alpha, d_ff=d_ff),
        grid=(n // tm,),
        in_specs=[tok, _const_spec((d, 2 * d_ff)), _const_spec((d_ff, d)),
                  _const_spec((1, d)), _const_spec((1, d))],
        out_specs=tok,
        out_shape=jax.ShapeDtypeStruct((n, d), F32),
        scratch_shapes=[pltpu.VMEM((tm, d_ff), BF16)],
        compiler_params=_params("arbitrary"),
        name="ffn_ln",
    )(x, w_gate_up.astype(BF16), w_down.astype(BF16), g.reshape(1, d), b.reshape(1, d))


def kernel(x, positions, ln_attn_g, ln_attn_b, ln_ffn_g, ln_ffn_b, w_qkv_diff, w_o_diff,
           lambda_q1, lambda_k1, lambda_q2, lambda_k2, subln_g, w_qkv_sb, w_o_sb,
           w_gate_up, w_down):
    b, s, d = x.shape
    depth = ln_attn_g.shape[0]
    alpha = (2 * depth) ** 0.25
    scale = HEAD_DIM ** -0.5
    tables = _rope_tables(positions)
    for i in range(depth):
        j = i // 2
        if i % 2 == 0:
            qt, k, vt = _qkv_proj(x, w_qkv_diff[j], scale, tables)
            h = _da_attention(qt, k, vt, lambda_q1[j], lambda_k1[j], lambda_q2[j],
                              lambda_k2[j], subln_g[j], _lambda_init(i))
            w_o = w_o_diff[j]
        else:
            qt, k, vt = _qkv_proj(x, w_qkv_sb[j], scale, None)
            h = _sb_attention(qt, k, vt)
            w_o = w_o_sb[j]
        x2 = _proj_ln(h.reshape(b * s, d), x.reshape(b * s, d), w_o,
                      ln_attn_g[i], ln_attn_b[i], alpha)
        x2 = _ffn_ln(x2, w_gate_up[i], w_down[i], ln_ffn_g[i], ln_ffn_b[i], alpha)
        x = x2.reshape(b, s, d)
    return x
```

```python
import functools
import math

import jax
import jax.numpy as jnp
from jax import lax
from jax.experimental import pallas as pl
from jax.experimental.pallas import tpu as pltpu

F32 = jnp.float32
BF16 = jnp.bfloat16

HEAD_DIM = 64
PAIR = 2 * HEAD_DIM
ROPE_DIM = HEAD_DIM // 4
ROPE_HALF = ROPE_DIM // 2
ROPE_THETA = 500000.0
LN_EPS = 1e-5
SUBLN_EPS = 1e-5
NEG_INF = -1e30
LOG2E = math.log2(math.e)
SB_SKIP_LOG = -105.0

BF16_SUBLANES = 16
TM_DENSE = 512
TQ_DA, TK_DA = 1024, 1024
TQ_SB, TK_SB = 256, 256
SB_GROUP = 4
FF_CHUNKS = 2
VMEM_LIMIT = 56 * 1024 * 1024


def _lambda_init(layer_idx):
    return 0.8 - 0.6 * math.exp(-0.3 * layer_idx)


def _nt(a, b):
    return lax.dot_general(a, b, (((1,), (1,)), ((), ())), preferred_element_type=F32)


def _dot(a, b):
    return jnp.dot(a, b, preferred_element_type=F32)


def _params(*sem):
    return pltpu.CompilerParams(dimension_semantics=sem, vmem_limit_bytes=VMEM_LIMIT)


def _const_spec(shape):
    nd = len(shape)
    return pl.BlockSpec(shape, lambda *_: (0,) * nd, pipeline_mode=pl.Buffered(1))


def _rope_table_kernel(pos_ref, invf_ref, cos_ref, sin_ref):
    ang = pos_ref[0].astype(F32) * invf_ref[...]
    cos_ref[0] = jnp.cos(ang)
    sin_ref[0] = jnp.sin(ang)


def _rope_tables(positions):
    b, s = positions.shape
    inv_freq = ROPE_THETA ** (-jnp.arange(0, ROPE_DIM, 2, dtype=F32) / ROPE_DIM)
    out = jax.ShapeDtypeStruct((b, ROPE_HALF, s), F32)
    return pl.pallas_call(
        _rope_table_kernel,
        grid=(b,),
        in_specs=[pl.BlockSpec((1, 1, s), lambda i: (i, 0, 0)),
                  _const_spec((ROPE_HALF, 1))],
        out_specs=[pl.BlockSpec((1, ROPE_HALF, s), lambda i: (i, 0, 0))] * 2,
        out_shape=[out, out],
        compiler_params=_params("arbitrary"),
        name="rope_tables",
    )(positions.reshape(b, 1, s), inv_freq.reshape(ROPE_HALF, 1))


def _rope_t(t, cos, sin):
    d, n = t.shape
    t4 = t.reshape(d // HEAD_DIM, HEAD_DIM // ROPE_HALF, ROPE_HALF, n)
    t1, t2 = t4[:, 0], t4[:, 1]
    r1 = t1 * cos - t2 * sin
    r2 = t2 * cos + t1 * sin
    out = jnp.concatenate([r1[:, None], r2[:, None], t4[:, 2:]], axis=1)
    return out.reshape(d, n)


def _qkv_kernel(*refs, rope):
    if rope:
        x_ref, wq_ref, wk_ref, wv_ref, cos_ref, sin_ref, q_ref, k_ref, v_ref = refs
    else:
        x_ref, wq_ref, wk_ref, wv_ref, q_ref, k_ref, v_ref = refs
    xb = x_ref[0].astype(BF16)
    qt = _nt(wq_ref[...], xb)
    v_ref[0] = _nt(wv_ref[...], xb).astype(BF16)
    if rope:
        cos, sin = cos_ref[0], sin_ref[0]
        q_ref[0] = _rope_t(qt, cos, sin).astype(BF16)
        kt = _rope_t(_nt(wk_ref[...], xb), cos, sin)
        k_ref[0] = kt.T.astype(BF16)
    else:
        q_ref[0] = qt.astype(BF16)
        k_ref[0] = _dot(xb, wk_ref[...]).astype(BF16)


def _qkv_proj(x, w_qkv, q_scale, tables):
    b, s, d = x.shape
    tm = min(TM_DENSE, s)
    rope = tables is not None
    wq_t = (w_qkv[:, :d] * q_scale).T.astype(BF16)
    wv_t = w_qkv[:, 2 * d:].T.astype(BF16)
    wk = w_qkv[:, d:2 * d]
    wk = (wk.T if rope else wk).astype(BF16)
    tok = pl.BlockSpec((1, tm, d), lambda i, j: (i, j, 0))
    tok_t = pl.BlockSpec((1, d, tm), lambda i, j: (i, 0, j))
    in_specs = [tok] + [_const_spec((d, d))] * 3
    args = [x, wq_t, wk, wv_t]
    if rope:
        in_specs += [pl.BlockSpec((1, ROPE_HALF, tm), lambda i, j: (i, 0, j))] * 2
        args += list(tables)
    return pl.pallas_call(
        functools.partial(_qkv_kernel, rope=rope),
        grid=(b, s // tm),
        in_specs=in_specs,
        out_specs=[tok_t, tok, tok_t],
        out_shape=[jax.ShapeDtypeStruct((b, d, s), BF16),
                   jax.ShapeDtypeStruct((b, s, d), BF16),
                   jax.ShapeDtypeStruct((b, d, s), BF16)],
        compiler_params=_params("arbitrary", "arbitrary"),
        name="qkv_rope" if rope else "qkv_plain",
    )(*args)


def _split_heads(qt, n):
    row = lax.broadcasted_iota(jnp.int32, qt.shape, 0)
    zero = jnp.zeros_like(qt)
    return [jnp.where((row >= h * HEAD_DIM) & (row < (h + 1) * HEAD_DIM), qt, zero)
            for h in range(n)]


def _da_attn_kernel(q_ref, k_ref, v_ref, lq1_ref, lk1_ref, lq2_ref, lk2_ref, g_ref, o_ref,
                    m_ref, alpha_ref, acc_ref, p_ref, *, tq, tk, lam_init):
    qi = pl.program_id(2)
    q_sub = _split_heads(q_ref[0], 2)
    ones = jnp.ones((BF16_SUBLANES, tk), BF16)

    m_ref[...] = jnp.full(m_ref.shape, NEG_INF, F32)
    acc_ref[...] = jnp.zeros(acc_ref.shape, F32)

    def scores(j, c):
        kt = k_ref[0, pl.ds(pl.multiple_of(j * tk, tk), tk), :]
        return _dot(kt, q_sub[c])

    def add_pending(j):
        vt = v_ref[0, :, pl.ds(pl.multiple_of(j * tk, tk), tk)]
        vt = jnp.concatenate([vt, ones], axis=0)
        for c in range(2):
            acc_ref[c] = alpha_ref[c] * acc_ref[c] + _dot(vt, p_ref[c])

    def softmax_step(sc, c, masked):
        if masked:
            key = lax.broadcasted_iota(jnp.int32, (tk, tq), 0)
            qry = lax.broadcasted_iota(jnp.int32, (tk, tq), 1)
            sc = jnp.where(key <= qry, sc, NEG_INF)
        m_old = m_ref[c]
        m_new = jnp.maximum(m_old, jnp.max(sc, axis=0, keepdims=True))
        alpha_ref[c] = jnp.exp2(m_old - m_new)
        p_ref[c] = jnp.exp2(sc - m_new).astype(BF16)
        m_ref[c] = m_new

    for c in range(2):
        softmax_step(scores(qi, c), c, masked=True)

    def body(j, carry):
        s0 = scores(j, 0)
        s1 = scores(j, 1)
        add_pending(jnp.where(j == 0, qi, j - 1))
        softmax_step(s0, 0, masked=False)
        softmax_step(s1, 1, masked=False)
        return carry

    lax.fori_loop(0, qi, body, 0)
    add_pending(jnp.maximum(qi - 1, 0))

    lam = (jnp.exp(jnp.sum(lq1_ref[...] * lk1_ref[...], axis=1, keepdims=True))
           - jnp.exp(jnp.sum(lq2_ref[...] * lk2_ref[...], axis=1, keepdims=True))
           + lam_init)
    o1 = acc_ref[0, :PAIR] / acc_ref[0, PAIR:PAIR + 1]
    o2 = acc_ref[1, :PAIR] / acc_ref[1, PAIR:PAIR + 1]
    o = o1 - lam * o2
    o = o * lax.rsqrt(jnp.mean(o * o, axis=0, keepdims=True) + SUBLN_EPS)
    o = o * g_ref[...] * (1.0 - lam_init)
    o_ref[0] = o.T.astype(o_ref.dtype)


def _da_attention(qt, k, vt, lq1, lk1, lq2, lk2, subln_g, lam_init):
    b, d, s = qt.shape
    tq = tk = min(TQ_DA, s)
    heads = d // PAIR
    vec = lambda a: a.reshape(1, HEAD_DIM).astype(F32)
    return pl.pallas_call(
        functools.partial(_da_attn_kernel, tq=tq, tk=tk, lam_init=lam_init),
        grid=(b, heads, s // tq),
        in_specs=[pl.BlockSpec((1, PAIR, tq), lambda i, h, j: (i, h, j)),
                  pl.BlockSpec((1, s, PAIR), lambda i, h, j: (i, 0, h)),
                  pl.BlockSpec((1, PAIR, s), lambda i, h, j: (i, h, 0)),
                  _const_spec((1, HEAD_DIM)), _const_spec((1, HEAD_DIM)),
                  _const_spec((1, HEAD_DIM)), _const_spec((1, HEAD_DIM)),
                  _const_spec((PAIR, 1))],
        out_specs=pl.BlockSpec((1, tq, PAIR), lambda i, h, j: (i, j, h)),
        out_shape=jax.ShapeDtypeStruct((b, s, d), BF16),
        scratch_shapes=[pltpu.VMEM((2, 1, tq), F32), pltpu.VMEM((2, 1, tq), F32),
                        pltpu.VMEM((2, PAIR + BF16_SUBLANES, tq), F32),
                        pltpu.VMEM((2, tk, tq), BF16)],
        compiler_params=_params("arbitrary", "arbitrary", "arbitrary"),
        name="da_attention",
    )(qt, k, vt, vec(lq1), vec(lk1), vec(lq2), vec(lk2),
      subln_g.reshape(PAIR, 1).astype(F32))


def _sb_attn_kernel(q_ref, k_ref, v_ref, tri_ref, o_ref, carry_ref, acc_ref, *, tq, tk, nh):
    qi = pl.program_id(2)
    q_sub = _split_heads(q_ref[0], nh)
    tri = tri_ref[...]
    n_diag = tq // tk

    carry_ref[...] = jnp.zeros(carry_ref.shape, F32)
    acc_ref[...] = jnp.zeros(acc_ref.shape, F32)

    def tile(j, masked):
        start = pl.multiple_of(j * tk, tk)
        kt = k_ref[0, pl.ds(start, tk), :]
        vt = v_ref[0, :, pl.ds(start, tk)]
        if masked:
            key = start + lax.broadcasted_iota(jnp.int32, (tk, tq), 0)
            qry = qi * tq + lax.broadcasted_iota(jnp.int32, (tk, tq), 1)
            mask = key < qry
        z = [_dot(kt, q_sub[h]) for h in range(nh)]
        split = []
        for h in range(nh):
            sp = jnp.maximum(z[h], 0.0) + jnp.log(1.0 + jnp.exp(-jnp.abs(z[h])))
            if masked:
                sp = jnp.where(mask, sp, 0.0)
            hi = sp.astype(BF16)
            lo = (sp - hi.astype(F32)).astype(BF16)
            split.append(jnp.concatenate([hi, lo], axis=0))
        incl = [_dot(tri, split[h]) for h in range(nh)]
        acts = []
        for h in range(nh):
            carry = carry_ref[h]
            w = jnp.exp((z[h] + carry) - incl[h])
            if masked:
                w = jnp.where(mask, w, 0.0)
            acts.append(w.astype(BF16))
            carry_ref[h] = carry - incl[h][0:1]
        for h in range(nh):
            acc_ref[h] += _dot(vt[h * HEAD_DIM:(h + 1) * HEAD_DIM], acts[h])

    for t in range(n_diag):
        tile(qi * n_diag + (n_diag - 1 - t), masked=True)

    def live():
        top = carry_ref[0]
        for h in range(1, nh):
            top = jnp.maximum(top, carry_ref[h])
        return jnp.max(top) > SB_SKIP_LOG

    def cond(state):
        j, alive = state
        return jnp.logical_and(j >= 0, alive)

    def body(state):
        j, _ = state
        tile(j, masked=False)
        return j - 1, live()

    lax.while_loop(cond, body, (qi * n_diag - 1, live()))

    o = jnp.concatenate([acc_ref[h] for h in range(nh)], axis=0)
    o_ref[0] = o.T.astype(o_ref.dtype)


def _sb_attention(qt, k, vt):
    b, d, s = qt.shape
    tq, tk = min(TQ_SB, s), min(TK_SB, s)
    nh = SB_GROUP
    width = nh * HEAD_DIM
    r = lax.broadcasted_iota(jnp.int32, (tk, 2 * tk), 0)
    c = lax.broadcasted_iota(jnp.int32, (tk, 2 * tk), 1) % tk
    tri = (c >= r).astype(BF16)
    return pl.pallas_call(
        functools.partial(_sb_attn_kernel, tq=tq, tk=tk, nh=nh),
        grid=(b, d // width, s // tq),
        in_specs=[pl.BlockSpec((1, width, tq), lambda i, h, j: (i, h, j)),
                  pl.BlockSpec((1, s, width), lambda i, h, j: (i, 0, h)),
                  pl.BlockSpec((1, width, s), lambda i, h, j: (i, h, 0)),
                  _const_spec((tk, 2 * tk))],
        out_specs=pl.BlockSpec((1, tq, width), lambda i, h, j: (i, j, h)),
        out_shape=jax.ShapeDtypeStruct((b, s, d), BF16),
        scratch_shapes=[pltpu.VMEM((nh, 1, tq), F32), pltpu.VMEM((nh, HEAD_DIM, tq), F32)],
        compiler_params=_params("arbitrary", "arbitrary", "arbitrary"),
        name="sb_attention",
    )(qt, k, vt, tri)


def _layer_norm(y, g, b):
    mu = jnp.mean(y, axis=-1, keepdims=True)
    yc = y - mu
    var = jnp.mean(yc * yc, axis=-1, keepdims=True)
    return yc * lax.rsqrt(var + LN_EPS) * g + b


def _proj_ln_kernel(h_ref, x_ref, w_ref, g_ref, b_ref, o_ref, *, alpha):
    y = alpha * x_ref[...] + _dot(h_ref[...], w_ref[...])
    o_ref[...] = _layer_norm(y, g_ref[...], b_ref[...])


def _proj_ln(h, x, w_o, g, b, alpha):
    n, d = x.shape
    tm = min(TM_DENSE, n)
    tok = pl.BlockSpec((tm, d), lambda i: (i, 0))
    return pl.pallas_call(
        functools.partial(_proj_ln_kernel, alpha=alpha),
        grid=(n // tm,),
        in_specs=[tok, tok, _const_spec((d, d)), _const_spec((1, d)), _const_spec((1, d))],
        out_specs=tok,
        out_shape=jax.ShapeDtypeStruct((n, d), F32),
        compiler_params=_params("arbitrary"),
        name="proj_ln",
    )(h, x, w_o.astype(BF16), g.reshape(1, d), b.reshape(1, d))


def _ffn_ln_kernel(x_ref, wgu_ref, wd_ref, g_ref, b_ref, o_ref, act_ref, *, alpha, d_ff):
    x = x_ref[...]
    xb = x.astype(BF16)
    ch = d_ff // FF_CHUNKS
    for c in range(FF_CHUNKS):
        gate = _dot(xb, wgu_ref[:, c * ch:(c + 1) * ch])
        up = _dot(xb, wgu_ref[:, d_ff + c * ch:d_ff + (c + 1) * ch])
        act_ref[:, c * ch:(c + 1) * ch] = (gate * jax.nn.sigmoid(gate) * up).astype(BF16)
    y = alpha * x + _dot(act_ref[...], wd_ref[...])
    o_ref[...] = _layer_norm(y, g_ref[...], b_ref[...])


def _ffn_ln(x, w_gate_up, w_down, g, b, alpha):
    n, d = x.shape
    d_ff = w_down.shape[0]
    tm = min(TM_DENSE, n)
    tok = pl.BlockSpec((tm, d), lambda i: (i, 0))
    return pl.pallas_call(
        functools.partial(_ffn_ln_kernel, alpha=alpha, d_ff=d_ff),
        grid=(n // tm,),
        in_specs=[tok, _const_spec((d, 2 * d_ff)), _const_spec((d_ff, d)),
                  _const_spec((1, d)), _const_spec((1, d))],
        out_specs=tok,
        out_shape=jax.ShapeDtypeStruct((n, d), F32),
        scratch_shapes=[pltpu.VMEM((tm, d_ff), BF16)],
        compiler_params=_params("arbitrary"),
        name="ffn_ln",
    )(x, w_gate_up.astype(BF16), w_down.astype(BF16), g.reshape(1, d), b.reshape(1, d))


def kernel(x, positions, ln_attn_g, ln_attn_b, ln_ffn_g, ln_ffn_b, w_qkv_diff, w_o_diff,
           lambda_q1, lambda_k1, lambda_q2, lambda_k2, subln_g, w_qkv_sb, w_o_sb,
           w_gate_up, w_down):
    b, s, d = x.shape
    depth = ln_attn_g.shape[0]
    alpha = (2 * depth) ** 0.25
    scale = HEAD_DIM ** -0.5
    tables = _rope_tables(positions)
    for i in range(depth):
        j = i // 2
        if i % 2 == 0:
            qt, k, vt = _qkv_proj(x, w_qkv_diff[j], scale * LOG2E, tables)
            h = _da_attention(qt, k, vt, lambda_q1[j], lambda_k1[j], lambda_q2[j],
                              lambda_k2[j], subln_g[j], _lambda_init(i))
            w_o = w_o_diff[j]
        else:
            qt, k, vt = _qkv_proj(x, w_qkv_sb[j], scale, None)
            h = _sb_attention(qt, k, vt)
            w_o = w_o_sb[j]
        x2 = _proj_ln(h.reshape(b * s, d), x.reshape(b * s, d), w_o,
                      ln_attn_g[i], ln_attn_b[i], alpha)
        x2 = _ffn_ln(x2, w_gate_up[i], w_down[i], ln_ffn_g[i], ln_ffn_b[i], alpha)
        x = x2.reshape(b, s, d)
    return x
```

```python
import functools
import math

import jax
import jax.numpy as jnp
from jax import lax
from jax.experimental import pallas as pl
from jax.experimental.pallas import tpu as pltpu

F32 = jnp.float32
BF16 = jnp.bfloat16

HEAD_DIM = 64
PAIR = 2 * HEAD_DIM
ROPE_DIM = HEAD_DIM // 4
ROPE_HALF = ROPE_DIM // 2
ROPE_THETA = 500000.0
LN_EPS = 1e-5
SUBLN_EPS = 1e-5
NEG_INF = -1e30
LOG2E = math.log2(math.e)
SB_SKIP_LOG = -105.0

BF16_SUBLANES = 16
TM_DENSE = 512
TM_FFN = 1024
T_DA = 1024
T_SB = 256
SB_GROUP = 4
FF_CHUNKS = 11
VMEM_LIMIT = 56 * 1024 * 1024


def _lambda_init(layer_idx):
    return 0.8 - 0.6 * math.exp(-0.3 * layer_idx)


def _nt(a, b):
    return lax.dot_general(a, b, (((1,), (1,)), ((), ())), preferred_element_type=F32)


def _dot(a, b):
    return jnp.dot(a, b, preferred_element_type=F32)


def _params(*sem):
    return pltpu.CompilerParams(dimension_semantics=sem, vmem_limit_bytes=VMEM_LIMIT)


def _const_spec(shape):
    nd = len(shape)
    return pl.BlockSpec(shape, lambda *_: (0,) * nd, pipeline_mode=pl.Buffered(1))


def _rope_table_kernel(pos_ref, invf_ref, cos_ref, sin_ref):
    ang = pos_ref[0].astype(F32) * invf_ref[...]
    cos_ref[0] = jnp.cos(ang)
    sin_ref[0] = jnp.sin(ang)


def _rope_tables(positions):
    b, s = positions.shape
    inv_freq = ROPE_THETA ** (-jnp.arange(0, ROPE_DIM, 2, dtype=F32) / ROPE_DIM)
    out = jax.ShapeDtypeStruct((b, ROPE_HALF, s), F32)
    return pl.pallas_call(
        _rope_table_kernel,
        grid=(b,),
        in_specs=[pl.BlockSpec((1, 1, s), lambda i: (i, 0, 0)),
                  _const_spec((ROPE_HALF, 1))],
        out_specs=[pl.BlockSpec((1, ROPE_HALF, s), lambda i: (i, 0, 0))] * 2,
        out_shape=[out, out],
        compiler_params=_params("arbitrary"),
        name="rope_tables",
    )(positions.reshape(b, 1, s), inv_freq.reshape(ROPE_HALF, 1))


def _rope_t(t, cos, sin):
    d, n = t.shape
    t4 = t.reshape(d // HEAD_DIM, HEAD_DIM // ROPE_HALF, ROPE_HALF, n)
    t1, t2 = t4[:, 0], t4[:, 1]
    r1 = t1 * cos - t2 * sin
    r2 = t2 * cos + t1 * sin
    out = jnp.concatenate([r1[:, None], r2[:, None], t4[:, 2:]], axis=1)
    return out.reshape(d, n)


def _qkv_kernel(*refs, rope):
    if rope:
        x_ref, wq_ref, wk_ref, wv_ref, cos_ref, sin_ref, q_ref, k_ref, v_ref = refs
    else:
        x_ref, wq_ref, wk_ref, wv_ref, q_ref, k_ref, v_ref = refs
    xb = x_ref[0].astype(BF16)
    qt = _nt(wq_ref[...], xb)
    v_ref[0] = _nt(wv_ref[...], xb).astype(BF16)
    if rope:
        cos, sin = cos_ref[0], sin_ref[0]
        q_ref[0] = _rope_t(qt, cos, sin).astype(BF16)
        kt = _rope_t(_nt(wk_ref[...], xb), cos, sin)
        k_ref[0] = kt.T.astype(BF16)
    else:
        q_ref[0] = qt.astype(BF16)
        k_ref[0] = _dot(xb, wk_ref[...]).astype(BF16)


def _qkv_proj(x, w_qkv, q_scale, tables):
    b, s, d = x.shape
    tm = min(TM_DENSE, s)
    rope = tables is not None
    wq_t = (w_qkv[:, :d] * q_scale).T.astype(BF16)
    wv_t = w_qkv[:, 2 * d:].T.astype(BF16)
    wk = w_qkv[:, d:2 * d]
    wk = (wk.T if rope else wk).astype(BF16)
    tok = pl.BlockSpec((1, tm, d), lambda i, j: (i, j, 0))
    tok_t = pl.BlockSpec((1, d, tm), lambda i, j: (i, 0, j))
    in_specs = [tok] + [_const_spec((d, d))] * 3
    args = [x, wq_t, wk, wv_t]
    if rope:
        in_specs += [pl.BlockSpec((1, ROPE_HALF, tm), lambda i, j: (i, 0, j))] * 2
        args += list(tables)
    return pl.pallas_call(
        functools.partial(_qkv_kernel, rope=rope),
        grid=(b, s // tm),
        in_specs=in_specs,
        out_specs=[tok_t, tok, tok_t],
        out_shape=[jax.ShapeDtypeStruct((b, d, s), BF16),
                   jax.ShapeDtypeStruct((b, s, d), BF16),
                   jax.ShapeDtypeStruct((b, d, s), BF16)],
        compiler_params=_params("arbitrary", "arbitrary"),
        name="qkv_rope" if rope else "qkv_plain",
    )(*args)


def _split_heads(qt, n):
    row = lax.broadcasted_iota(jnp.int32, qt.shape, 0)
    zero = jnp.zeros_like(qt)
    return [jnp.where((row >= h * HEAD_DIM) & (row < (h + 1) * HEAD_DIM), qt, zero)
            for h in range(n)]


def _da_attn_kernel(q_ref, k_ref, v_ref, lq1_ref, lk1_ref, lq2_ref, lk2_ref, g_ref, o_ref,
                    m_ref, alpha_ref, acc_ref, p_ref, *, t, lam_init):
    qi = pl.program_id(2)
    q_sub = _split_heads(q_ref[0], 2)
    ones = jnp.ones((BF16_SUBLANES, t), BF16)

    m_ref[...] = jnp.full(m_ref.shape, NEG_INF, F32)
    acc_ref[...] = jnp.zeros(acc_ref.shape, F32)

    def scores(j, c):
        kt = k_ref[0, pl.ds(pl.multiple_of(j * t, t), t), :]
        return _dot(kt, q_sub[c])

    def add_pending(j):
        vt = v_ref[0, :, pl.ds(pl.multiple_of(j * t, t), t)]
        vt = jnp.concatenate([vt, ones], axis=0)
        for c in range(2):
            acc_ref[c] = alpha_ref[c] * acc_ref[c] + _dot(vt, p_ref[c])

    def softmax_step(sc, c, masked):
        if masked:
            key = lax.broadcasted_iota(jnp.int32, (t, t), 0)
            qry = lax.broadcasted_iota(jnp.int32, (t, t), 1)
            sc = jnp.where(key <= qry, sc, NEG_INF)
        m_old = m_ref[c]
        m_new = jnp.maximum(m_old, jnp.max(sc, axis=0, keepdims=True))
        alpha_ref[c] = jnp.exp2(m_old - m_new)
        p_ref[c] = jnp.exp2(sc - m_new).astype(BF16)
        m_ref[c] = m_new

    for c in range(2):
        softmax_step(scores(qi, c), c, masked=True)

    def body(j, carry):
        s0 = scores(j, 0)
        s1 = scores(j, 1)
        add_pending(jnp.where(j == 0, qi, j - 1))
        softmax_step(s0, 0, masked=False)
        softmax_step(s1, 1, masked=False)
        return carry

    lax.fori_loop(0, qi, body, 0)
    add_pending(jnp.maximum(qi - 1, 0))

    lam = (jnp.exp(jnp.sum(lq1_ref[...] * lk1_ref[...], axis=1, keepdims=True))
           - jnp.exp(jnp.sum(lq2_ref[...] * lk2_ref[...], axis=1, keepdims=True))
           + lam_init)
    o1 = acc_ref[0, :PAIR] / acc_ref[0, PAIR:PAIR + 1]
    o2 = acc_ref[1, :PAIR] / acc_ref[1, PAIR:PAIR + 1]
    o = o1 - lam * o2
    o = o * lax.rsqrt(jnp.mean(o * o, axis=0, keepdims=True) + SUBLN_EPS)
    o = o * g_ref[...] * (1.0 - lam_init)
    o_ref[0] = o.T.astype(o_ref.dtype)


def _da_attention(qt, k, vt, lq1, lk1, lq2, lk2, subln_g, lam_init):
    b, d, s = qt.shape
    t = min(T_DA, s)
    heads = d // PAIR
    vec = lambda a: a.reshape(1, HEAD_DIM).astype(F32)
    return pl.pallas_call(
        functools.partial(_da_attn_kernel, t=t, lam_init=lam_init),
        grid=(b, heads, s // t),
        in_specs=[pl.BlockSpec((1, PAIR, t), lambda i, h, j: (i, h, j)),
                  pl.BlockSpec((1, s, PAIR), lambda i, h, j: (i, 0, h)),
                  pl.BlockSpec((1, PAIR, s), lambda i, h, j: (i, h, 0)),
                  _const_spec((1, HEAD_DIM)), _const_spec((1, HEAD_DIM)),
                  _const_spec((1, HEAD_DIM)), _const_spec((1, HEAD_DIM)),
                  _const_spec((PAIR, 1))],
        out_specs=pl.BlockSpec((1, t, PAIR), lambda i, h, j: (i, j, h)),
        out_shape=jax.ShapeDtypeStruct((b, s, d), BF16),
        scratch_shapes=[pltpu.VMEM((2, 1, t), F32), pltpu.VMEM((2, 1, t), F32),
                        pltpu.VMEM((2, PAIR + BF16_SUBLANES, t), F32),
                        pltpu.VMEM((2, t, t), BF16)],
        compiler_params=_params("arbitrary", "arbitrary", "arbitrary"),
        name="da_attention",
    )(qt, k, vt, vec(lq1), vec(lk1), vec(lq2), vec(lk2),
      subln_g.reshape(PAIR, 1).astype(F32))


def _sb_attn_kernel(q_ref, k_ref, v_ref, tri_ref, o_ref, carry_ref, acc_ref, *, t, nh):
    qi = pl.program_id(2)
    q_sub = _split_heads(q_ref[0], nh)
    tri = tri_ref[...]

    carry_ref[...] = jnp.zeros(carry_ref.shape, F32)
    acc_ref[...] = jnp.zeros(acc_ref.shape, F32)

    def block(tiles):
        starts = [pl.multiple_of(j * t, t) for j, _ in tiles]
        kts = [k_ref[0, pl.ds(st, t), :] for st in starts]
        vts = [v_ref[0, :, pl.ds(st, t)] for st in starts]
        key = lax.broadcasted_iota(jnp.int32, (t, t), 0)
        qry = lax.broadcasted_iota(jnp.int32, (t, t), 1)
        mask = key < qry
        z = [[_dot(kt, q_sub[h]) for h in range(nh)] for kt in kts]
        split = []
        for i, (_, diagonal) in enumerate(tiles):
            row = []
            for h in range(nh):
                zz = z[i][h]
                sp = jnp.maximum(zz, 0.0) + jnp.log(1.0 + jnp.exp2(jnp.abs(zz) * -LOG2E))
                if diagonal:
                    sp = jnp.where(mask, sp, 0.0)
                row.append(sp.astype(BF16))
            split.append(row)
        incl = [[_dot(tri, split[i][h]) for h in range(nh)] for i in range(len(tiles))]
        acts = []
        carry = [carry_ref[h] for h in range(nh)]
        for i, (_, diagonal) in enumerate(tiles):
            row = []
            for h in range(nh):
                w = jnp.exp((z[i][h] + carry[h]) - incl[i][h])
                if diagonal:
                    w = jnp.where(mask, w, 0.0)
                row.append(w.astype(BF16))
                carry[h] = carry[h] - incl[i][h][0:1]
            acts.append(row)
        for h in range(nh):
            carry_ref[h] = carry[h]
            pv = _dot(vts[0][h * HEAD_DIM:(h + 1) * HEAD_DIM], acts[0][h])
            for i in range(1, len(tiles)):
                pv = pv + _dot(vts[i][h * HEAD_DIM:(h + 1) * HEAD_DIM], acts[i][h])
            acc_ref[h] += pv

    pl.when(qi == 0)(lambda: block([(qi, True)]))
    pl.when(qi > 0)(lambda: block([(qi, True), (qi - 1, False)]))

    def live():
        top = carry_ref[0]
        for h in range(1, nh):
            top = jnp.maximum(top, carry_ref[h])
        return jnp.max(top) > SB_SKIP_LOG

    def cond(state):
        j, alive = state
        return jnp.logical_and(j >= 0, alive)

    def body(state):
        j, _ = state
        block([(j, False)])
        return j - 1, live()

    lax.while_loop(cond, body, (qi - 2, live()))

    o = jnp.concatenate([acc_ref[h] for h in range(nh)], axis=0)
    o_ref[0] = o.T.astype(o_ref.dtype)


def _sb_attention(qt, k, vt):
    b, d, s = qt.shape
    t = min(T_SB, s)
    nh = SB_GROUP
    width = nh * HEAD_DIM
    r = lax.broadcasted_iota(jnp.int32, (t, t), 0)
    c = lax.broadcasted_iota(jnp.int32, (t, t), 1)
    tri = (c >= r).astype(BF16)
    return pl.pallas_call(
        functools.partial(_sb_attn_kernel, t=t, nh=nh),
        grid=(b, d // width, s // t),
        in_specs=[pl.BlockSpec((1, width, t), lambda i, h, j: (i, h, j)),
                  pl.BlockSpec((1, s, width), lambda i, h, j: (i, 0, h)),
                  pl.BlockSpec((1, width, s), lambda i, h, j: (i, h, 0)),
                  _const_spec((t, t))],
        out_specs=pl.BlockSpec((1, t, width), lambda i, h, j: (i, j, h)),
        out_shape=jax.ShapeDtypeStruct((b, s, d), BF16),
        scratch_shapes=[pltpu.VMEM((nh, 1, t), F32), pltpu.VMEM((nh, HEAD_DIM, t), F32)],
        compiler_params=_params("arbitrary", "arbitrary", "arbitrary"),
        name="sb_attention",
    )(qt, k, vt, tri)


def _layer_norm(y, g, b):
    mu = jnp.mean(y, axis=-1, keepdims=True)
    yc = y - mu
    var = jnp.mean(yc * yc, axis=-1, keepdims=True)
    return yc * lax.rsqrt(var + LN_EPS) * g + b


def _proj_ffn_kernel(h_ref, x_ref, wo_ref, g1_ref, b1_ref, wgu_ref, wd_ref, g2_ref, b2_ref,
                     o_ref, act_ref, *, alpha, d_ff):
    y = alpha * x_ref[...] + _dot(h_ref[...], wo_ref[...])
    x1 = _layer_norm(y, g1_ref[...], b1_ref[...])
    xb = x1.astype(BF16)
    ch = d_ff // FF_CHUNKS
    for c in range(FF_CHUNKS):
        gate = _dot(xb, wgu_ref[:, c * ch:(c + 1) * ch])
        up = _dot(xb, wgu_ref[:, d_ff + c * ch:d_ff + (c + 1) * ch])
        act_ref[:, c * ch:(c + 1) * ch] = (gate * jax.nn.sigmoid(gate) * up).astype(BF16)
    y = alpha * x1 + _dot(act_ref[...], wd_ref[...])
    o_ref[...] = _layer_norm(y, g2_ref[...], b2_ref[...])


def _proj_ffn(h, x, w_o, g1, b1, w_gate_up, w_down, g2, b2, alpha):
    n, d = x.shape
    d_ff = w_down.shape[0]
    tm = min(TM_FFN, n)
    tok = pl.BlockSpec((tm, d), lambda i: (i, 0))
    vec = _const_spec((1, d))
    return pl.pallas_call(
        functools.partial(_proj_ffn_kernel, alpha=alpha, d_ff=d_ff),
        grid=(n // tm,),
        in_specs=[tok, tok, _const_spec((d, d)), vec, vec,
                  _const_spec((d, 2 * d_ff)), _const_spec((d_ff, d)), vec, vec],
        out_specs=tok,
        out_shape=jax.ShapeDtypeStruct((n, d), F32),
        scratch_shapes=[pltpu.VMEM((tm, d_ff), BF16)],
        compiler_params=_params("arbitrary"),
        name="proj_ffn",
    )(h, x, w_o.astype(BF16), g1.reshape(1, d), b1.reshape(1, d),
      w_gate_up.astype(BF16), w_down.astype(BF16), g2.reshape(1, d), b2.reshape(1, d))


def kernel(x, positions, ln_attn_g, ln_attn_b, ln_ffn_g, ln_ffn_b, w_qkv_diff, w_o_diff,
           lambda_q1, lambda_k1, lambda_q2, lambda_k2, subln_g, w_qkv_sb, w_o_sb,
           w_gate_up, w_down):
    b, s, d = x.shape
    depth = ln_attn_g.shape[0]
    alpha = (2 * depth) ** 0.25
    scale = HEAD_DIM ** -0.5
    tables = _rope_tables(positions)
    for i in range(depth):
        j = i // 2
        if i % 2 == 0:
            qt, k, vt = _qkv_proj(x, w_qkv_diff[j], scale * LOG2E, tables)
            h = _da_attention(qt, k, vt, lambda_q1[j], lambda_k1[j], lambda_q2[j],
                              lambda_k2[j], subln_g[j], _lambda_init(i))
            w_o = w_o_diff[j]
        else:
            qt, k, vt = _qkv_proj(x, w_qkv_sb[j], scale, None)
            h = _sb_attention(qt, k, vt)
            w_o = w_o_sb[j]
        x2 = _proj_ffn(h.reshape(b * s, d), x.reshape(b * s, d), w_o,
                       ln_attn_g[i], ln_attn_b[i], w_gate_up[i], w_down[i],
                       ln_ffn_g[i], ln_ffn_b[i], alpha)
        x = x2.reshape(b, s, d)
    return x
```

```python
import functools
import math

import jax
import jax.numpy as jnp
from jax import lax
from jax.experimental import pallas as pl
from jax.experimental.pallas import tpu as pltpu

F32 = jnp.float32
BF16 = jnp.bfloat16

HEAD_DIM = 64
PAIR = 2 * HEAD_DIM
ROPE_DIM = HEAD_DIM // 4
ROPE_HALF = ROPE_DIM // 2
ROPE_THETA = 500000.0
LN_EPS = 1e-5
SUBLN_EPS = 1e-5
NEG_INF = -1e30
LOG2E = math.log2(math.e)
SB_SKIP_LOG = -105.0

BF16_SUBLANES = 16
TM_DENSE = 512
TM_FFN = 1024
T_DA = 1024
T_SB = 256
SB_GROUP = 4
FF_CHUNKS = 11
VMEM_LIMIT = 56 * 1024 * 1024


def _lambda_init(layer_idx):
    return 0.8 - 0.6 * math.exp(-0.3 * layer_idx)


def _nt(a, b):
    return lax.dot_general(a, b, (((1,), (1,)), ((), ())), preferred_element_type=F32)


def _dot(a, b):
    return jnp.dot(a, b, preferred_element_type=F32)


def _params(*sem):
    return pltpu.CompilerParams(dimension_semantics=sem, vmem_limit_bytes=VMEM_LIMIT)


def _const_spec(shape):
    nd = len(shape)
    return pl.BlockSpec(shape, lambda *_: (0,) * nd, pipeline_mode=pl.Buffered(1))


def _rope_table_kernel(pos_ref, invf_ref, cos_ref, sin_ref):
    ang = pos_ref[0].astype(F32) * invf_ref[...]
    cos_ref[0] = jnp.cos(ang)
    sin_ref[0] = jnp.sin(ang)


def _rope_tables(positions):
    b, s = positions.shape
    inv_freq = ROPE_THETA ** (-jnp.arange(0, ROPE_DIM, 2, dtype=F32) / ROPE_DIM)
    out = jax.ShapeDtypeStruct((b, ROPE_HALF, s), F32)
    return pl.pallas_call(
        _rope_table_kernel,
        grid=(b,),
        in_specs=[pl.BlockSpec((1, 1, s), lambda i: (i, 0, 0)),
                  _const_spec((ROPE_HALF, 1))],
        out_specs=[pl.BlockSpec((1, ROPE_HALF, s), lambda i: (i, 0, 0))] * 2,
        out_shape=[out, out],
        compiler_params=_params("arbitrary"),
        name="rope_tables",
    )(positions.reshape(b, 1, s), inv_freq.reshape(ROPE_HALF, 1))


def _rope_t(t, cos, sin):
    d, n = t.shape
    t4 = t.reshape(d // HEAD_DIM, HEAD_DIM // ROPE_HALF, ROPE_HALF, n)
    t1, t2 = t4[:, 0], t4[:, 1]
    r1 = t1 * cos - t2 * sin
    r2 = t2 * cos + t1 * sin
    out = jnp.concatenate([r1[:, None], r2[:, None], t4[:, 2:]], axis=1)
    return out.reshape(d, n)


def _qkv_kernel(*refs, rope):
    if rope:
        x_ref, wq_ref, wk_ref, wv_ref, cos_ref, sin_ref, q_ref, k_ref, v_ref = refs
    else:
        x_ref, wq_ref, wk_ref, wv_ref, q_ref, k_ref, v_ref = refs
    xb = x_ref[0].astype(BF16)
    qt = _nt(wq_ref[...], xb)
    v_ref[0] = _nt(wv_ref[...], xb).astype(BF16)
    if rope:
        cos, sin = cos_ref[0], sin_ref[0]
        q_ref[0] = _rope_t(qt, cos, sin).astype(BF16)
        kt = _rope_t(_nt(wk_ref[...], xb), cos, sin)
        k_ref[0] = kt.T.astype(BF16)
    else:
        q_ref[0] = qt.astype(BF16)
        k_ref[0] = _dot(xb, wk_ref[...]).astype(BF16)


def _qkv_proj(x, w_qkv, q_scale, tables):
    b, s, d = x.shape
    tm = min(TM_DENSE, s)
    rope = tables is not None
    wq_t = (w_qkv[:, :d] * q_scale).T.astype(BF16)
    wv_t = w_qkv[:, 2 * d:].T.astype(BF16)
    wk = w_qkv[:, d:2 * d]
    wk = (wk.T if rope else wk).astype(BF16)
    tok = pl.BlockSpec((1, tm, d), lambda i, j: (i, j, 0))
    tok_t = pl.BlockSpec((1, d, tm), lambda i, j: (i, 0, j))
    in_specs = [tok] + [_const_spec((d, d))] * 3
    args = [x, wq_t, wk, wv_t]
    if rope:
        in_specs += [pl.BlockSpec((1, ROPE_HALF, tm), lambda i, j: (i, 0, j))] * 2
        args += list(tables)
    return pl.pallas_call(
        functools.partial(_qkv_kernel, rope=rope),
        grid=(b, s // tm),
        in_specs=in_specs,
        out_specs=[tok_t, tok, tok_t],
        out_shape=[jax.ShapeDtypeStruct((b, d, s), BF16),
                   jax.ShapeDtypeStruct((b, s, d), BF16),
                   jax.ShapeDtypeStruct((b, d, s), BF16)],
        compiler_params=_params("arbitrary", "arbitrary"),
        name="qkv_rope" if rope else "qkv_plain",
    )(*args)


def _split_heads(qt, n):
    row = lax.broadcasted_iota(jnp.int32, qt.shape, 0)
    zero = jnp.zeros_like(qt)
    return [jnp.where((row >= h * HEAD_DIM) & (row < (h + 1) * HEAD_DIM), qt, zero)
            for h in range(n)]


def _da_attn_kernel(q_ref, k_ref, v_ref, lq1_ref, lk1_ref, lq2_ref, lk2_ref, g_ref, o_ref,
                    m_ref, alpha_ref, acc_ref, p_ref, *, t, lam_init):
    qi = pl.program_id(2)
    q_sub = _split_heads(q_ref[0], 2)
    half = t // 2
    ones = jnp.ones((BF16_SUBLANES, t), BF16)

    acc_ref[...] = jnp.zeros(acc_ref.shape, F32)

    def scores(j, c):
        kt = k_ref[0, pl.ds(pl.multiple_of(j * t, t), t), :]
        return _dot(kt, q_sub[c])

    def add_pending(j):
        vt = v_ref[0, :, pl.ds(pl.multiple_of(j * t, t), t)]
        vt = jnp.concatenate([vt, ones], axis=0)
        for c in range(2):
            acc_ref[c] = alpha_ref[c] * acc_ref[c] + _dot(vt, p_ref[c])

    def softmax_step(sc, c):
        m_old = m_ref[c]
        m_new = jnp.maximum(m_old, jnp.max(sc, axis=0, keepdims=True))
        alpha_ref[c] = jnp.exp2(m_old - m_new)
        p_ref[c] = jnp.exp2(sc - m_new).astype(BF16)
        m_ref[c] = m_new

    def diagonal_step(c):
        k0 = k_ref[0, pl.ds(pl.multiple_of(qi * t, t), half), :]
        k1 = k_ref[0, pl.ds(pl.multiple_of(qi * t + half, half), half), :]
        sa = _dot(k0, q_sub[c])
        sb = _dot(k1, q_sub[c][:, half:])
        key = lax.broadcasted_iota(jnp.int32, (half, half), 0)
        qry = lax.broadcasted_iota(jnp.int32, (half, half), 1)
        causal = key <= qry
        sa_l = jnp.where(causal, sa[:, :half], NEG_INF)
        sa_r = sa[:, half:]
        sb = jnp.where(causal, sb, NEG_INF)
        m_l = jnp.max(sa_l, axis=0, keepdims=True)
        m_r = jnp.maximum(jnp.max(sa_r, axis=0, keepdims=True),
                          jnp.max(sb, axis=0, keepdims=True))
        p_l, p_r, p_b = jnp.exp2(sa_l - m_l), jnp.exp2(sa_r - m_r), jnp.exp2(sb - m_r)
        p_ref[c, :half, :half] = p_l.astype(BF16)
        p_ref[c, :half, half:] = p_r.astype(BF16)
        p_ref[c, half:, :half] = jnp.zeros((half, half), BF16)
        p_ref[c, half:, half:] = p_b.astype(BF16)
        m_ref[c] = jnp.concatenate([m_l, m_r], axis=1)
        alpha_ref[c] = jnp.zeros((1, t), F32)

    for c in range(2):
        diagonal_step(c)

    def body(j, carry):
        s0 = scores(j, 0)
        s1 = scores(j, 1)
        add_pending(jnp.where(j == 0, qi, j - 1))
        softmax_step(s0, 0)
        softmax_step(s1, 1)
        return carry

    lax.fori_loop(0, qi, body, 0)
    add_pending(jnp.maximum(qi - 1, 0))

    lam = (jnp.exp(jnp.sum(lq1_ref[...] * lk1_ref[...], axis=1, keepdims=True))
           - jnp.exp(jnp.sum(lq2_ref[...] * lk2_ref[...], axis=1, keepdims=True))
           + lam_init)
    o1 = acc_ref[0, :PAIR] / acc_ref[0, PAIR:PAIR + 1]
    o2 = acc_ref[1, :PAIR] / acc_ref[1, PAIR:PAIR + 1]
    o = o1 - lam * o2
    o = o * lax.rsqrt(jnp.mean(o * o, axis=0, keepdims=True) + SUBLN_EPS)
    o = o * g_ref[...] * (1.0 - lam_init)
    o_ref[0] = o.T.astype(o_ref.dtype)


def _da_attention(qt, k, vt, lq1, lk1, lq2, lk2, subln_g, lam_init):
    b, d, s = qt.shape
    t = min(T_DA, s)
    heads = d // PAIR
    vec = lambda a: a.reshape(1, HEAD_DIM).astype(F32)
    return pl.pallas_call(
        functools.partial(_da_attn_kernel, t=t, lam_init=lam_init),
        grid=(b, heads, s // t),
        in_specs=[pl.BlockSpec((1, PAIR, t), lambda i, h, j: (i, h, j)),
                  pl.BlockSpec((1, s, PAIR), lambda i, h, j: (i, 0, h)),
                  pl.BlockSpec((1, PAIR, s), lambda i, h, j: (i, h, 0)),
                  _const_spec((1, HEAD_DIM)), _const_spec((1, HEAD_DIM)),
                  _const_spec((1, HEAD_DIM)), _const_spec((1, HEAD_DIM)),
                  _const_spec((PAIR, 1))],
        out_specs=pl.BlockSpec((1, t, PAIR), lambda i, h, j: (i, j, h)),
        out_shape=jax.ShapeDtypeStruct((b, s, d), BF16),
        scratch_shapes=[pltpu.VMEM((2, 1, t), F32), pltpu.VMEM((2, 1, t), F32),
                        pltpu.VMEM((2, PAIR + BF16_SUBLANES, t), F32),
                        pltpu.VMEM((2, t, t), BF16)],
        compiler_params=_params("arbitrary", "arbitrary", "arbitrary"),
        name="da_attention",
    )(qt, k, vt, vec(lq1), vec(lk1), vec(lq2), vec(lk2),
      subln_g.reshape(PAIR, 1).astype(F32))


def _sb_attn_kernel(q_ref, k_ref, v_ref, tri_ref, o_ref, carry_ref, acc_ref, *, t, nh):
    qi = pl.program_id(2)
    q_sub = _split_heads(q_ref[0], nh)
    tri = tri_ref[...]

    carry_ref[...] = jnp.zeros(carry_ref.shape, F32)
    acc_ref[...] = jnp.zeros(acc_ref.shape, F32)

    def block(tiles):
        starts = [pl.multiple_of(j * t, t) for j, _ in tiles]
        kts = [k_ref[0, pl.ds(st, t), :] for st in starts]
        vts = [v_ref[0, :, pl.ds(st, t)] for st in starts]
        key = lax.broadcasted_iota(jnp.int32, (t, t), 0)
        qry = lax.broadcasted_iota(jnp.int32, (t, t), 1)
        mask = key < qry
        z = [[_dot(kt, q_sub[h]) for h in range(nh)] for kt in kts]
        split = []
        for i, (_, diagonal) in enumerate(tiles):
            row = []
            for h in range(nh):
                zz = z[i][h]
                sp = jnp.maximum(zz, 0.0) + jnp.log(1.0 + jnp.exp2(jnp.abs(zz) * -LOG2E))
                if diagonal:
                    sp = jnp.where(mask, sp, 0.0)
                row.append(sp.astype(BF16))
            split.append(row)
        incl = [[_dot(tri, split[i][h]) for h in range(nh)] for i in range(len(tiles))]
        acts = []
        carry = [carry_ref[h] for h in range(nh)]
        for i, (_, diagonal) in enumerate(tiles):
            row = []
            for h in range(nh):
                w = jnp.exp((z[i][h] + carry[h]) - incl[i][h])
                if diagonal:
                    w = jnp.where(mask, w, 0.0)
                row.append(w.astype(BF16))
                carry[h] = carry[h] - incl[i][h][0:1]
            acts.append(row)
        for h in range(nh):
            carry_ref[h] = carry[h]
            pv = _dot(vts[0][h * HEAD_DIM:(h + 1) * HEAD_DIM], acts[0][h])
            for i in range(1, len(tiles)):
                pv = pv + _dot(vts[i][h * HEAD_DIM:(h + 1) * HEAD_DIM], acts[i][h])
            acc_ref[h] += pv

    pl.when(qi == 0)(lambda: block([(qi, True)]))
    pl.when(qi > 0)(lambda: block([(qi, True), (qi - 1, False)]))

    def live():
        top = carry_ref[0]
        for h in range(1, nh):
            top = jnp.maximum(top, carry_ref[h])
        return jnp.max(top) > SB_SKIP_LOG

    def cond(state):
        j, alive = state
        return jnp.logical_and(j >= 0, alive)

    def body(state):
        j, _ = state
        block([(j, False)])
        return j - 1, live()

    lax.while_loop(cond, body, (qi - 2, live()))

    o = jnp.concatenate([acc_ref[h] for h in range(nh)], axis=0)
    o_ref[0] = o.T.astype(o_ref.dtype)


def _sb_attention(qt, k, vt):
    b, d, s = qt.shape
    t = min(T_SB, s)
    nh = SB_GROUP
    width = nh * HEAD_DIM
    r = lax.broadcasted_iota(jnp.int32, (t, t), 0)
    c = lax.broadcasted_iota(jnp.int32, (t, t), 1)
    tri = (c >= r).astype(BF16)
    return pl.pallas_call(
        functools.partial(_sb_attn_kernel, t=t, nh=nh),
        grid=(b, d // width, s // t),
        in_specs=[pl.BlockSpec((1, width, t), lambda i, h, j: (i, h, j)),
                  pl.BlockSpec((1, s, width), lambda i, h, j: (i, 0, h)),
                  pl.BlockSpec((1, width, s), lambda i, h, j: (i, h, 0)),
                  _const_spec((t, t))],
        out_specs=pl.BlockSpec((1, t, width), lambda i, h, j: (i, j, h)),
        out_shape=jax.ShapeDtypeStruct((b, s, d), BF16),
        scratch_shapes=[pltpu.VMEM((nh, 1, t), F32), pltpu.VMEM((nh, HEAD_DIM, t), F32)],
        compiler_params=_params("arbitrary", "arbitrary", "arbitrary"),
        name="sb_attention",
    )(qt, k, vt, tri)


def _layer_norm(y, g, b):
    mu = jnp.mean(y, axis=-1, keepdims=True)
    yc = y - mu
    var = jnp.mean(yc * yc, axis=-1, keepdims=True)
    return yc * lax.rsqrt(var + LN_EPS) * g + b


def _proj_ffn_kernel(h_ref, x_ref, wo_ref, g1_ref, b1_ref, wgu_ref, wd_ref, g2_ref, b2_ref,
                     o_ref, act_ref, *, alpha, d_ff):
    y = alpha * x_ref[...] + _dot(h_ref[...], wo_ref[...])
    x1 = _layer_norm(y, g1_ref[...], b1_ref[...])
    xb = x1.astype(BF16)
    ch = d_ff // FF_CHUNKS
    for c in range(FF_CHUNKS):
        gate = _dot(xb, wgu_ref[:, c * ch:(c + 1) * ch])
        up = _dot(xb, wgu_ref[:, d_ff + c * ch:d_ff + (c + 1) * ch])
        act_ref[:, c * ch:(c + 1) * ch] = (gate * jax.nn.sigmoid(gate) * up).astype(BF16)
    y = alpha * x1 + _dot(act_ref[...], wd_ref[...])
    o_ref[...] = _layer_norm(y, g2_ref[...], b2_ref[...])


def _proj_ffn(h, x, w_o, g1, b1, w_gate_up, w_down, g2, b2, alpha):
    n, d = x.shape
    d_ff = w_down.shape[0]
    tm = min(TM_FFN, n)
    tok = pl.BlockSpec((tm, d), lambda i: (i, 0))
    vec = _const_spec((1, d))
    return pl.pallas_call(
        functools.partial(_proj_ffn_kernel, alpha=alpha, d_ff=d_ff),
        grid=(n // tm,),
        in_specs=[tok, tok, _const_spec((d, d)), vec, vec,
                  _const_spec((d, 2 * d_ff)), _const_spec((d_ff, d)), vec, vec],
        out_specs=tok,
        out_shape=jax.ShapeDtypeStruct((n, d), F32),
        scratch_shapes=[pltpu.VMEM((tm, d_ff), BF16)],
        compiler_params=_params("arbitrary"),
        name="proj_ffn",
    )(h, x, w_o.astype(BF16), g1.reshape(1, d), b1.reshape(1, d),
      w_gate_up.astype(BF16), w_down.astype(BF16), g2.reshape(1, d), b2.reshape(1, d))


def kernel(x, positions, ln_attn_g, ln_attn_b, ln_ffn_g, ln_ffn_b, w_qkv_diff, w_o_diff,
           lambda_q1, lambda_k1, lambda_q2, lambda_k2, subln_g, w_qkv_sb, w_o_sb,
           w_gate_up, w_down):
    b, s, d = x.shape
    depth = ln_attn_g.shape[0]
    alpha = (2 * depth) ** 0.25
    scale = HEAD_DIM ** -0.5
    tables = _rope_tables(positions)
    for i in range(depth):
        j = i // 2
        if i % 2 == 0:
            qt, k, vt = _qkv_proj(x, w_qkv_diff[j], scale * LOG2E, tables)
            h = _da_attention(qt, k, vt, lambda_q1[j], lambda_k1[j], lambda_q2[j],
                              lambda_k2[j], subln_g[j], _lambda_init(i))
            w_o = w_o_diff[j]
        else:
            qt, k, vt = _qkv_proj(x, w_qkv_sb[j], scale, None)
            h = _sb_attention(qt, k, vt)
            w_o = w_o_sb[j]
        x2 = _proj_ffn(h.reshape(b * s, d), x.reshape(b * s, d), w_o,
                       ln_attn_g[i], ln_attn_b[i], w_gate_up[i], w_down[i],
                       ln_ffn_g[i], ln_ffn_b[i], alpha)
        x = x2.reshape(b, s, d)
    return x
```

```python
import functools
import math

import jax
import jax.numpy as jnp
from jax import lax
from jax.experimental import pallas as pl
from jax.experimental.pallas import tpu as pltpu

F32 = jnp.float32
BF16 = jnp.bfloat16

HEAD_DIM = 64
PAIR = 2 * HEAD_DIM
ROPE_DIM = HEAD_DIM // 4
ROPE_HALF = ROPE_DIM // 2
ROPE_THETA = 500000.0
LN_EPS = 1e-5
SUBLN_EPS = 1e-5
NEG_INF = -1e30
LOG2E = math.log2(math.e)
SB_SKIP_LOG = -105.0

BF16_SUBLANES = 16
TM_DENSE = 512
TM_FFN = 1024
T_DA = 1024
DA_GROUP = 2
T_SB = 256
SB_GROUP = 4
FF_CHUNKS = 11
VMEM_LIMIT = 56 * 1024 * 1024


def _lambda_init(layer_idx):
    return 0.8 - 0.6 * math.exp(-0.3 * layer_idx)


def _nt(a, b):
    return lax.dot_general(a, b, (((1,), (1,)), ((), ())), preferred_element_type=F32)


def _dot(a, b):
    return jnp.dot(a, b, preferred_element_type=F32)


def _params(*sem):
    return pltpu.CompilerParams(dimension_semantics=sem, vmem_limit_bytes=VMEM_LIMIT)


def _const_spec(shape):
    nd = len(shape)
    return pl.BlockSpec(shape, lambda *_: (0,) * nd, pipeline_mode=pl.Buffered(1))


def _rope_table_kernel(pos_ref, invf_ref, cos_ref, sin_ref):
    ang = pos_ref[0].astype(F32) * invf_ref[...]
    cos_ref[0] = jnp.cos(ang)
    sin_ref[0] = jnp.sin(ang)


def _rope_tables(positions):
    b, s = positions.shape
    inv_freq = ROPE_THETA ** (-jnp.arange(0, ROPE_DIM, 2, dtype=F32) / ROPE_DIM)
    out = jax.ShapeDtypeStruct((b, ROPE_HALF, s), F32)
    return pl.pallas_call(
        _rope_table_kernel,
        grid=(b,),
        in_specs=[pl.BlockSpec((1, 1, s), lambda i: (i, 0, 0)),
                  _const_spec((ROPE_HALF, 1))],
        out_specs=[pl.BlockSpec((1, ROPE_HALF, s), lambda i: (i, 0, 0))] * 2,
        out_shape=[out, out],
        compiler_params=_params("arbitrary"),
        name="rope_tables",
    )(positions.reshape(b, 1, s), inv_freq.reshape(ROPE_HALF, 1))


def _rope_t(t, cos, sin):
    d, n = t.shape
    t4 = t.reshape(d // HEAD_DIM, HEAD_DIM // ROPE_HALF, ROPE_HALF, n)
    t1, t2 = t4[:, 0], t4[:, 1]
    r1 = t1 * cos - t2 * sin
    r2 = t2 * cos + t1 * sin
    out = jnp.concatenate([r1[:, None], r2[:, None], t4[:, 2:]], axis=1)
    return out.reshape(d, n)


def _qkv_kernel(*refs, rope):
    if rope:
        x_ref, wq_ref, wk_ref, wv_ref, cos_ref, sin_ref, q_ref, k_ref, v_ref = refs
    else:
        x_ref, wq_ref, wk_ref, wv_ref, q_ref, k_ref, v_ref = refs
    xb = x_ref[0].astype(BF16)
    qt = _nt(wq_ref[...], xb)
    v_ref[0] = _nt(wv_ref[...], xb).astype(BF16)
    if rope:
        cos, sin = cos_ref[0], sin_ref[0]
        q_ref[0] = _rope_t(qt, cos, sin).astype(BF16)
        kt = _rope_t(_nt(wk_ref[...], xb), cos, sin)
        k_ref[0] = kt.T.astype(BF16)
    else:
        q_ref[0] = qt.astype(BF16)
        k_ref[0] = _dot(xb, wk_ref[...]).astype(BF16)


def _qkv_proj(x, w_qkv, q_scale, tables):
    b, s, d = x.shape
    tm = min(TM_DENSE, s)
    rope = tables is not None
    wq_t = (w_qkv[:, :d] * q_scale).T.astype(BF16)
    wv_t = w_qkv[:, 2 * d:].T.astype(BF16)
    wk = w_qkv[:, d:2 * d]
    wk = (wk.T if rope else wk).astype(BF16)
    tok = pl.BlockSpec((1, tm, d), lambda i, j: (i, j, 0))
    tok_t = pl.BlockSpec((1, d, tm), lambda i, j: (i, 0, j))
    in_specs = [tok] + [_const_spec((d, d))] * 3
    args = [x, wq_t, wk, wv_t]
    if rope:
        in_specs += [pl.BlockSpec((1, ROPE_HALF, tm), lambda i, j: (i, 0, j))] * 2
        args += list(tables)
    return pl.pallas_call(
        functools.partial(_qkv_kernel, rope=rope),
        grid=(b, s // tm),
        in_specs=in_specs,
        out_specs=[tok_t, tok, tok_t],
        out_shape=[jax.ShapeDtypeStruct((b, d, s), BF16),
                   jax.ShapeDtypeStruct((b, s, d), BF16),
                   jax.ShapeDtypeStruct((b, d, s), BF16)],
        compiler_params=_params("arbitrary", "arbitrary"),
        name="qkv_rope" if rope else "qkv_plain",
    )(*args)


def _split_heads(qt, n):
    row = lax.broadcasted_iota(jnp.int32, qt.shape, 0)
    zero = jnp.zeros_like(qt)
    return [jnp.where((row >= h * HEAD_DIM) & (row < (h + 1) * HEAD_DIM), qt, zero)
            for h in range(n)]


def _da_attn_kernel(q_ref, k_ref, v_ref, lq1_ref, lk1_ref, lq2_ref, lk2_ref, g_ref, o_ref,
                    m_ref, alpha_ref, acc_ref, p_ref, *, t, nh, lam_init):
    qi = pl.program_id(2)
    nc = 2 * nh
    q_sub = _split_heads(q_ref[0], nc)
    half = t // 2
    ones = jnp.ones((BF16_SUBLANES, t), BF16)

    acc_ref[...] = jnp.zeros(acc_ref.shape, F32)

    def scores(j, c):
        kt = k_ref[0, pl.ds(pl.multiple_of(j * t, t), t), :]
        return _dot(kt, q_sub[c])

    def add_pending(j):
        v_all = v_ref[0, :, pl.ds(pl.multiple_of(j * t, t), t)]
        for h in range(nh):
            vt = jnp.concatenate([v_all[h * PAIR:(h + 1) * PAIR], ones], axis=0)
            for c in (2 * h, 2 * h + 1):
                acc_ref[c] = alpha_ref[c] * acc_ref[c] + _dot(vt, p_ref[c])

    def softmax_step(sc, c):
        m_old = m_ref[c]
        m_new = jnp.maximum(m_old, jnp.max(sc, axis=0, keepdims=True))
        alpha_ref[c] = jnp.exp2(m_old - m_new)
        p_ref[c] = jnp.exp2(sc - m_new).astype(BF16)
        m_ref[c] = m_new

    def diagonal_scores(c):
        k0 = k_ref[0, pl.ds(pl.multiple_of(qi * t, t), half), :]
        k1 = k_ref[0, pl.ds(pl.multiple_of(qi * t + half, half), half), :]
        return _dot(k0, q_sub[c]), _dot(k1, q_sub[c][:, half:])

    def diagonal_step(c, sa, sb):
        key = lax.broadcasted_iota(jnp.int32, (half, half), 0)
        qry = lax.broadcasted_iota(jnp.int32, (half, half), 1)
        causal = key <= qry
        sa_l = jnp.where(causal, sa[:, :half], NEG_INF)
        sa_r = sa[:, half:]
        sb = jnp.where(causal, sb, NEG_INF)
        m_l = jnp.max(sa_l, axis=0, keepdims=True)
        m_r = jnp.maximum(jnp.max(sa_r, axis=0, keepdims=True),
                          jnp.max(sb, axis=0, keepdims=True))
        p_l, p_r, p_b = jnp.exp2(sa_l - m_l), jnp.exp2(sa_r - m_r), jnp.exp2(sb - m_r)
        p_ref[c, :half, :half] = p_l.astype(BF16)
        p_ref[c, :half, half:] = p_r.astype(BF16)
        p_ref[c, half:, :half] = jnp.zeros((half, half), BF16)
        p_ref[c, half:, half:] = p_b.astype(BF16)
        m_ref[c] = jnp.concatenate([m_l, m_r], axis=1)
        alpha_ref[c] = jnp.zeros((1, t), F32)

    diag = [diagonal_scores(c) for c in range(nc)]
    for c in range(nc):
        diagonal_step(c, *diag[c])

    def body(j, carry):
        s = [scores(j, c) for c in range(nc)]
        add_pending(jnp.where(j == 0, qi, j - 1))
        for c in range(nc):
            softmax_step(s[c], c)
        return carry

    lax.fori_loop(0, qi, body, 0)
    add_pending(jnp.maximum(qi - 1, 0))

    lam = (jnp.exp(jnp.sum(lq1_ref[...] * lk1_ref[...], axis=1, keepdims=True))
           - jnp.exp(jnp.sum(lq2_ref[...] * lk2_ref[...], axis=1, keepdims=True))
           + lam_init)
    outs = []
    for h in range(nh):
        o1 = acc_ref[2 * h, :PAIR] / acc_ref[2 * h, PAIR:PAIR + 1]
        o2 = acc_ref[2 * h + 1, :PAIR] / acc_ref[2 * h + 1, PAIR:PAIR + 1]
        o = o1 - lam * o2
        o = o * lax.rsqrt(jnp.mean(o * o, axis=0, keepdims=True) + SUBLN_EPS)
        outs.append(o * g_ref[...] * (1.0 - lam_init))
    o_ref[0] = jnp.concatenate(outs, axis=0).T.astype(o_ref.dtype)


def _da_attention(qt, k, vt, lq1, lk1, lq2, lk2, subln_g, lam_init):
    b, d, s = qt.shape
    t = min(T_DA, s)
    nh = DA_GROUP
    width = nh * PAIR
    vec = lambda a: a.reshape(1, HEAD_DIM).astype(F32)
    return pl.pallas_call(
        functools.partial(_da_attn_kernel, t=t, nh=nh, lam_init=lam_init),
        grid=(b, d // width, s // t),
        in_specs=[pl.BlockSpec((1, width, t), lambda i, h, j: (i, h, j)),
                  pl.BlockSpec((1, s, width), lambda i, h, j: (i, 0, h)),
                  pl.BlockSpec((1, width, s), lambda i, h, j: (i, h, 0)),
                  _const_spec((1, HEAD_DIM)), _const_spec((1, HEAD_DIM)),
                  _const_spec((1, HEAD_DIM)), _const_spec((1, HEAD_DIM)),
                  _const_spec((PAIR, 1))],
        out_specs=pl.BlockSpec((1, t, width), lambda i, h, j: (i, j, h)),
        out_shape=jax.ShapeDtypeStruct((b, s, d), BF16),
        scratch_shapes=[pltpu.VMEM((2 * nh, 1, t), F32), pltpu.VMEM((2 * nh, 1, t), F32),
                        pltpu.VMEM((2 * nh, PAIR + BF16_SUBLANES, t), F32),
                        pltpu.VMEM((2 * nh, t, t), BF16)],
        compiler_params=_params("arbitrary", "arbitrary", "arbitrary"),
        name="da_attention",
    )(qt, k, vt, vec(lq1), vec(lk1), vec(lq2), vec(lk2),
      subln_g.reshape(PAIR, 1).astype(F32))


def _sb_attn_kernel(q_ref, k_ref, v_ref, tri_ref, o_ref, carry_ref, acc_ref, *, t, nh):
    qi = pl.program_id(2)
    q_sub = _split_heads(q_ref[0], nh)
    tri = tri_ref[...]

    carry_ref[...] = jnp.zeros(carry_ref.shape, F32)
    acc_ref[...] = jnp.zeros(acc_ref.shape, F32)

    def block(tiles):
        starts = [pl.multiple_of(j * t, t) for j, _ in tiles]
        kts = [k_ref[0, pl.ds(st, t), :] for st in starts]
        vts = [v_ref[0, :, pl.ds(st, t)] for st in starts]
        key = lax.broadcasted_iota(jnp.int32, (t, t), 0)
        qry = lax.broadcasted_iota(jnp.int32, (t, t), 1)
        mask = key < qry
        z = [[_dot(kt, q_sub[h]) for h in range(nh)] for kt in kts]
        split = []
        for i, (_, diagonal) in enumerate(tiles):
            row = []
            for h in range(nh):
                zz = z[i][h]
                sp = jnp.maximum(zz, 0.0) + jnp.log(1.0 + jnp.exp2(jnp.abs(zz) * -LOG2E))
                if diagonal:
                    sp = jnp.where(mask, sp, 0.0)
                row.append(sp.astype(BF16))
            split.append(row)
        incl = [[_dot(tri, split[i][h]) for h in range(nh)] for i in range(len(tiles))]
        acts = []
        carry = [carry_ref[h] for h in range(nh)]
        for i, (_, diagonal) in enumerate(tiles):
            row = []
            for h in range(nh):
                w = jnp.exp((z[i][h] + carry[h]) - incl[i][h])
                if diagonal:
                    w = jnp.where(mask, w, 0.0)
                row.append(w.astype(BF16))
                carry[h] = carry[h] - incl[i][h][0:1]
            acts.append(row)
        for h in range(nh):
            carry_ref[h] = carry[h]
            pv = _dot(vts[0][h * HEAD_DIM:(h + 1) * HEAD_DIM], acts[0][h])
            for i in range(1, len(tiles)):
                pv = pv + _dot(vts[i][h * HEAD_DIM:(h + 1) * HEAD_DIM], acts[i][h])
            acc_ref[h] += pv

    pl.when(qi == 0)(lambda: block([(qi, True)]))
    pl.when(qi > 0)(lambda: block([(qi, True), (qi - 1, False)]))

    def live():
        top = carry_ref[0]
        for h in range(1, nh):
            top = jnp.maximum(top, carry_ref[h])
        return jnp.max(top) > SB_SKIP_LOG

    def cond(state):
        j, alive = state
        return jnp.logical_and(j >= 0, alive)

    def body(state):
        j, _ = state
        block([(j, False)])
        return j - 1, live()

    lax.while_loop(cond, body, (qi - 2, live()))

    o = jnp.concatenate([acc_ref[h] for h in range(nh)], axis=0)
    o_ref[0] = o.T.astype(o_ref.dtype)


def _sb_attention(qt, k, vt):
    b, d, s = qt.shape
    t = min(T_SB, s)
    nh = SB_GROUP
    width = nh * HEAD_DIM
    r = lax.broadcasted_iota(jnp.int32, (t, t), 0)
    c = lax.broadcasted_iota(jnp.int32, (t, t), 1)
    tri = (c >= r).astype(BF16)
    return pl.pallas_call(
        functools.partial(_sb_attn_kernel, t=t, nh=nh),
        grid=(b, d // width, s // t),
        in_specs=[pl.BlockSpec((1, width, t), lambda i, h, j: (i, h, j)),
                  pl.BlockSpec((1, s, width), lambda i, h, j: (i, 0, h)),
                  pl.BlockSpec((1, width, s), lambda i, h, j: (i, h, 0)),
                  _const_spec((t, t))],
        out_specs=pl.BlockSpec((1, t, width), lambda i, h, j: (i, j, h)),
        out_shape=jax.ShapeDtypeStruct((b, s, d), BF16),
        scratch_shapes=[pltpu.VMEM((nh, 1, t), F32), pltpu.VMEM((nh, HEAD_DIM, t), F32)],
        compiler_params=_params("arbitrary", "arbitrary", "arbitrary"),
        name="sb_attention",
    )(qt, k, vt, tri)


def _layer_norm(y, g, b):
    mu = jnp.mean(y, axis=-1, keepdims=True)
    yc = y - mu
    var = jnp.mean(yc * yc, axis=-1, keepdims=True)
    return yc * lax.rsqrt(var + LN_EPS) * g + b


def _proj_ffn_kernel(h_ref, x_ref, wo_ref, g1_ref, b1_ref, wgu_ref, wd_ref, g2_ref, b2_ref,
                     o_ref, act_ref, *, alpha, d_ff):
    y = alpha * x_ref[...] + _dot(h_ref[...], wo_ref[...])
    x1 = _layer_norm(y, g1_ref[...], b1_ref[...])
    xb = x1.astype(BF16)
    ch = d_ff // FF_CHUNKS
    for c in range(FF_CHUNKS):
        gate = _dot(xb, wgu_ref[:, c * ch:(c + 1) * ch])
        up = _dot(xb, wgu_ref[:, d_ff + c * ch:d_ff + (c + 1) * ch])
        act_ref[:, c * ch:(c + 1) * ch] = (gate * jax.nn.sigmoid(gate) * up).astype(BF16)
    y = alpha * x1 + _dot(act_ref[...], wd_ref[...])
    o_ref[...] = _layer_norm(y, g2_ref[...], b2_ref[...])


def _proj_ffn(h, x, w_o, g1, b1, w_gate_up, w_down, g2, b2, alpha):
    n, d = x.shape
    d_ff = w_down.shape[0]
    tm = min(TM_FFN, n)
    tok = pl.BlockSpec((tm, d), lambda i: (i, 0))
    vec = _const_spec((1, d))
    return pl.pallas_call(
        functools.partial(_proj_ffn_kernel, alpha=alpha, d_ff=d_ff),
        grid=(n // tm,),
        in_specs=[tok, tok, _const_spec((d, d)), vec, vec,
                  _const_spec((d, 2 * d_ff)), _const_spec((d_ff, d)), vec, vec],
        out_specs=tok,
        out_shape=jax.ShapeDtypeStruct((n, d), F32),
        scratch_shapes=[pltpu.VMEM((tm, d_ff), BF16)],
        compiler_params=_params("arbitrary"),
        name="proj_ffn",
    )(h, x, w_o.astype(BF16), g1.reshape(1, d), b1.reshape(1, d),
      w_gate_up.astype(BF16), w_down.astype(BF16), g2.reshape(1, d), b2.reshape(1, d))


def kernel(x, positions, ln_attn_g, ln_attn_b, ln_ffn_g, ln_ffn_b, w_qkv_diff, w_o_diff,
           lambda_q1, lambda_k1, lambda_q2, lambda_k2, subln_g, w_qkv_sb, w_o_sb,
           w_gate_up, w_down):
    b, s, d = x.shape
    depth = ln_attn_g.shape[0]
    alpha = (2 * depth) ** 0.25
    scale = HEAD_DIM ** -0.5
    tables = _rope_tables(positions)
    for i in range(depth):
        j = i // 2
        if i % 2 == 0:
            qt, k, vt = _qkv_proj(x, w_qkv_diff[j], scale * LOG2E, tables)
            h = _da_attention(qt, k, vt, lambda_q1[j], lambda_k1[j], lambda_q2[j],
                              lambda_k2[j], subln_g[j], _lambda_init(i))
            w_o = w_o_diff[j]
        else:
            qt, k, vt = _qkv_proj(x, w_qkv_sb[j], scale, None)
            h = _sb_attention(qt, k, vt)
            w_o = w_o_sb[j]
        x2 = _proj_ffn(h.reshape(b * s, d), x.reshape(b * s, d), w_o,
                       ln_attn_g[i], ln_attn_b[i], w_gate_up[i], w_down[i],
                       ln_ffn_g[i], ln_ffn_b[i], alpha)
        x = x2.reshape(b, s, d)
    return x
```
